```python
import math
import jax
import jax.numpy as jnp
from jax import lax
import numpy as np

D_MODEL = 4096
BATCH = 4
SEQ = 2048
DEPTH = 4
DEC_BATCH = 8
DEC_SEQ = 4
PAST_LEN = 8192
PAGE_SIZE = 128

D_A = D_MODEL // 4
D_B = D_MODEL // 4
D_C = D_MODEL // 4
N_HEADS = D_MODEL // 256
HEAD_DIM = 128
N_KV = 4
Q_PER_KV = N_HEADS // N_KV
D_D = N_HEADS * HEAD_DIM
MIX_W = D_A + D_B + D_C + D_D
KV_W = 6 * N_KV * HEAD_DIM
N_IN = 3 * D_A + 2 * D_B + D_C + D_D + KV_W + 3 * N_HEADS
CONV_A = 3
CONV_B = 31
POOL_WINDOWS = (2, 4, 8, 16)
POOL_MAX = 16
N_POOL_GROUPS = 4
D_CG = D_C // N_POOL_GROUPS
CMP_LEN = 32
CMP_STRIDE = 16
SEL_LEN = 64
SEL_TOP = 16
N_LOCAL_SEL = 2
WINDOW = 512
Q_BLOCK = 128
FORCE_SCORE = 1e4
NEG_INF = -1e30
N_BUCKETS = 32
MAX_DISTANCE = 128
N_EXPERTS = 64
TOP_K = 8
D_EXPERT = D_MODEL // 8
D_SHARED = D_MODEL // 8
RMS_EPS = 1e-6
LN_EPS = 1e-5

kernel_name = 'hybrid_conv_pool_nsa_moe_step'

F32 = jnp.float32


def rmsnorm(x, w):
    xf = x.astype(F32)
    y = xf * lax.rsqrt(jnp.mean(xf * xf, axis=-1, keepdims=True) + RMS_EPS)
    return (y * w.astype(F32)).astype(x.dtype)


def layernorm(x, g, b):
    xf = x.astype(F32)
    mu = jnp.mean(xf, axis=-1, keepdims=True)
    var = jnp.mean(jnp.square(xf - mu), axis=-1, keepdims=True)
    return ((xf - mu) * lax.rsqrt(var + LN_EPS) * g.astype(F32) + b.astype(F32)).astype(x.dtype)


def masked_softmax(s, mask, axis):
    s = jnp.where(mask, s, NEG_INF)
    m = jnp.max(s, axis=axis, keepdims=True)
    p = jnp.where(mask, jnp.exp(s - m), 0.0)
    return p / jnp.maximum(jnp.sum(p, axis=axis, keepdims=True), 1e-30)


def rel_bucket(dist):
    n = jnp.maximum(dist, 0)
    max_exact = N_BUCKETS // 2
    large = max_exact + (jnp.log(jnp.maximum(n, 1).astype(F32) / max_exact)
                         / math.log(MAX_DISTANCE / max_exact) * (N_BUCKETS - max_exact)).astype(jnp.int32)
    large = jnp.minimum(large, N_BUCKETS - 1)
    return jnp.where(n < max_exact, n, large)


def causal_dwconv(ext, w, T):
    C = ext.shape[-1]
    return lax.conv_general_dilated(ext, w[:, None, :].astype(ext.dtype), window_strides=(1,), padding='VALID',
                                    dimension_numbers=('NWC', 'WIO', 'NWC'), feature_group_count=C)


def multiscale_pool(ext, u, pos0, w_grp, scale):
    B, T, C = u.shape
    P = ext.shape[1] - T
    cs = jnp.concatenate([jnp.zeros((B, 1, C), F32), jnp.cumsum(ext.astype(F32), axis=1)], axis=1)
    pos = pos0 + jnp.arange(T)
    outs = []
    for gi, w in enumerate(POOL_WINDOWS):
        sl = slice(gi * D_CG, (gi + 1) * D_CG)
        win_sum = cs[:, P + 1:P + 1 + T, sl] - cs[:, P + 1 - w:P + 1 - w + T, sl]
        cnt = jnp.minimum(w, pos + 1).astype(F32)
        outs.append(win_sum / cnt[None, :, None])
    pooled = (jnp.concatenate(outs, axis=-1) - u.astype(F32)).astype(u.dtype)
    mixed = jnp.einsum('btgc,gce->btge', pooled.reshape(B, T, N_POOL_GROUPS, D_CG), w_grp).reshape(B, T, C)
    return mixed * scale


def cmp_sel_cover(nc, ns):
    c0 = jnp.arange(nc)[:, None] * CMP_STRIDE
    s0 = jnp.arange(ns)[None, :] * SEL_LEN
    ov = jnp.minimum(c0 + CMP_LEN, s0 + SEL_LEN) - jnp.maximum(c0, s0)
    return jnp.maximum(ov, 0).astype(F32) / CMP_LEN


def nsa_core(q, g, q_pos, kcb, vcb, cmp_end, cover, ks_b, vs_b, kw, vw, kw_pos, rel_bias):
    B, T, G, R, Dh = q.shape
    scale = Dh ** -0.5
    table = rel_bias.astype(F32).reshape(N_BUCKETS, G, R)

    def bias_2d(dist):
        return jnp.transpose(table[rel_bucket(dist)], (2, 3, 0, 1))

    d_c = q_pos[:, None] - cmp_end[None, :]
    s_c = jnp.einsum('btgrd,bngd->bgrtn', q, kcb, preferred_element_type=F32) * scale + bias_2d(d_c)
    p_c = masked_softmax(s_c, cmp_end[None, :] <= q_pos[:, None], -1)
    o_c = jnp.einsum('bgrtn,bngd->btgrd', p_c.astype(vcb.dtype), vcb)

    imp = jnp.einsum('bgrtn,ns->bgts', p_c, cover)
    ns = cover.shape[1]
    blk = jnp.arange(ns)[None, :]
    cur = (q_pos // SEL_LEN)[:, None]
    forced = (blk == 0) | ((blk <= cur) & (blk > cur - N_LOCAL_SEL))
    valid = blk * SEL_LEN <= q_pos[:, None]
    score = jnp.where(forced, FORCE_SCORE, jnp.where(valid, imp, -1.0))
    _, idx = lax.top_k(score, min(SEL_TOP, ns))
    n_sel = idx.shape[-1]
    bi = jnp.arange(B)[:, None, None, None]
    gi = jnp.arange(G)[None, :, None, None]
    ks = ks_b[bi, gi, idx]
    vs = vs_b[bi, gi, idx]
    kpos = idx[..., None] * SEL_LEN + jnp.arange(SEL_LEN)
    d_s = q_pos[None, None, :, None, None] - kpos
    b_s = table[rel_bucket(d_s), jnp.arange(G)[None, :, None, None, None]]
    s_s = jnp.einsum('btgrd,bgtnkd->bgtnkr', q, ks, preferred_element_type=F32) * scale + b_s
    M = n_sel * SEL_LEN
    p_s = masked_softmax(s_s.reshape(B, G, T, M, R), (d_s >= 0).reshape(B, G, T, M, 1), -2)
    o_s = jnp.einsum('bgtmr,bgtmd->btgrd', p_s.astype(vs.dtype), vs.reshape(B, G, T, M, Dh))

    d_w = q_pos[:, None] - kw_pos[None, :]
    m_w = (d_w >= 0) & (d_w < WINDOW) & (kw_pos[None, :] >= 0)
    s_w = jnp.einsum('btgrd,bkgd->bgrtk', q, kw, preferred_element_type=F32) * scale + bias_2d(d_w)
    p_w = masked_softmax(s_w, m_w, -1)
    o_w = jnp.einsum('bgrtk,bkgd->btgrd', p_w.astype(vw.dtype), vw)

    return g[..., 0:1] * o_c + g[..., 1:2] * o_s + g[..., 2:3] * o_w


def nsa_mixer(q, g, kv_full, kw_rows, kw_pos0, q_pos0, rel_bias, cmp_pe, cmp_w1, cmp_w2):
    B, T, G, R, Dh = q.shape
    L = kv_full.shape[1]
    r = CMP_LEN // CMP_STRIDE
    nch = L // CMP_STRIDE
    nc = nch - r + 1
    chunks = kv_full[:, :nch * CMP_STRIDE, :2].reshape(B, nch, CMP_STRIDE, 2, G, Dh)
    blocks = jnp.concatenate([chunks[:, m:m + nc] for m in range(r)], axis=2)
    blocks = blocks + cmp_pe[None, None, :, :, None, :]
    flat = jnp.transpose(blocks, (0, 1, 3, 4, 2, 5)).reshape(B, nc, 2, G, CMP_LEN * Dh)
    comp = jnp.einsum('bnsge,sed->bnsgd', jax.nn.silu(jnp.einsum('bnsgf,sfe->bnsge', flat, cmp_w1)), cmp_w2)
    kcb, vcb = comp[:, :, 0], comp[:, :, 1]
    cmp_end = jnp.arange(nc) * CMP_STRIDE + (CMP_LEN - 1)
    ns = -(-L // SEL_LEN)
    sel = jnp.pad(kv_full[:, :, 2:], ((0, 0), (0, ns * SEL_LEN - L), (0, 0), (0, 0), (0, 0)))
    sel = jnp.transpose(sel.reshape(B, ns, SEL_LEN, 2, G, Dh), (3, 0, 4, 1, 2, 5))
    cover = cmp_sel_cover(nc, ns)
    bq = Q_BLOCK if T % Q_BLOCK == 0 else T
    W = kw_rows.shape[1]
    band = WINDOW + bq

    def run_block(i):
        s0 = i * bq
        qb = lax.dynamic_slice_in_dim(q, s0, bq, axis=1)
        gb = lax.dynamic_slice_in_dim(g, s0, bq, axis=1)
        q_pos = q_pos0 + s0 + jnp.arange(bq)
        if W <= band:
            kwb, kw_pos = kw_rows, kw_pos0 + jnp.arange(W)
        else:
            st = jnp.maximum(q_pos0 + s0 - WINDOW - kw_pos0, 0)
            kwb = lax.dynamic_slice_in_dim(kw_rows, st, band, axis=1)
            kw_pos = kw_pos0 + st + jnp.arange(band)
        return nsa_core(qb, gb, q_pos, kcb, vcb, cmp_end, cover, sel[0], sel[1],
                        kwb[:, :, 0], kwb[:, :, 1], kw_pos, rel_bias)

    out = lax.map(run_block, jnp.arange(T // bq))
    return jnp.moveaxis(out, 0, 1).reshape(B, T, G * R * Dh)


def in_split_points():
    sizes = (3 * D_A, 2 * D_B, D_C, D_D, KV_W)
    pts, acc = [], 0
    for s in sizes:
        acc += s
        pts.append(acc)
    return pts


def mixer_sublayer(h, l, pos0, prev, win_keep, prm):
    kv_past, win_prev, ca_prev, cb_prev, pool_prev = prev
    B, T, _ = h.shape
    z = jnp.einsum('btd,de->bte', h, prm['w_in'][l])
    za, zb, zc, zq, zkv, zg = jnp.split(z, in_split_points(), axis=-1)
    a_b, a_c, a_h = jnp.split(za, 3, axis=-1)
    ext_a = jnp.concatenate([ca_prev, a_c * a_h], axis=1)
    y_a = a_b * causal_dwconv(ext_a, prm['conv_a_w'][l], T)
    b_lin, b_gate = jnp.split(zb, 2, axis=-1)
    ext_b = jnp.concatenate([cb_prev, b_lin * jax.nn.sigmoid(b_gate)], axis=1)
    cb = causal_dwconv(ext_b, prm['conv_b_w'][l], T) + prm['conv_b_b'][l]
    y_b = jax.nn.silu(layernorm(cb, prm['ln_b_g'][l], prm['ln_b_b'][l]))
    ext_c = jnp.concatenate([pool_prev, zc], axis=1)
    y_c = multiscale_pool(ext_c, zc, pos0, prm['pool_w'][l], prm['pool_scale'][l])
    q = zq.reshape(B, T, N_KV, Q_PER_KV, HEAD_DIM)
    g = jax.nn.sigmoid(zg.reshape(B, T, N_KV, Q_PER_KV, 3))
    kv = zkv.reshape(B, T, 6, N_KV, HEAD_DIM)
    kv_full = jnp.concatenate([kv_past, kv[:, :, :4]], axis=1)
    win_rows = jnp.concatenate([win_prev, kv[:, :, 4:]], axis=1)
    y_d = nsa_mixer(q, g, kv_full, win_rows, pos0 - win_prev.shape[1], pos0, prm['rel_bias'],
                    prm['cmp_pe'][l], prm['cmp_w1'][l], prm['cmp_w2'][l])
    wb = prm['w_branch'][l]
    o1, o2, o3 = D_A, D_A + D_B, D_A + D_B + D_C
    gates = jax.nn.sigmoid(jnp.einsum('btd,de->bte', h, prm['w_mgate'][l])).reshape(B, T, 4, D_MODEL)
    merged = (gates[:, :, 0] * jnp.einsum('btc,cd->btd', y_a, wb[:o1])
              + gates[:, :, 1] * jnp.einsum('btc,cd->btd', y_b, wb[o1:o2])
              + gates[:, :, 2] * jnp.einsum('btc,cd->btd', y_c, wb[o2:o3])
              + gates[:, :, 3] * jnp.einsum('btc,cd->btd', y_d, wb[o3:]))
    y = jnp.einsum('btd,de->bte', merged, prm['w_out'][l])
    state = (kv[:, :, :4], win_rows[:, -win_keep:], ext_a[:, -(CONV_A - 1):],
             ext_b[:, -(CONV_B - 1):], ext_c[:, -(POOL_MAX - 1):])
    return y, state


def moe_ffn(h, w_router, b_router, w_eg, w_eu, w_ed, w_sg, w_su, w_sd):
    aff = jax.nn.sigmoid(jnp.einsum('btd,de->bte', h, w_router, preferred_element_type=F32))
    _, idx = lax.top_k(aff + b_router.astype(F32), TOP_K)
    top_aff = jnp.take_along_axis(aff, idx, axis=-1)
    wts = top_aff / jnp.sum(top_aff, axis=-1, keepdims=True)
    gate = jnp.einsum('btk,btke->bte', wts, jax.nn.one_hot(idx, N_EXPERTS, dtype=F32)).astype(h.dtype)
    hid = jax.nn.silu(jnp.einsum('btd,edf->btef', h, w_eg)) * jnp.einsum('btd,edf->btef', h, w_eu)
    routed = jnp.einsum('btef,efd->btd', hid * gate[..., None], w_ed)
    shared = jnp.einsum('btf,fd->btd', jax.nn.silu(h @ w_sg) * (h @ w_su), w_sd)
    return routed + shared


def run_trunk(x, c, pos0, prev_fn, win_keep, prm):
    B = x.shape[0]
    cs = jax.nn.silu(c)
    outs = ([], [], [], [], [])
    for l in range(DEPTH):
        mod = (jnp.einsum('bd,de->be', cs, prm['ada_w'][l]) + prm['ada_b'][l]).reshape(B, 1, 6, D_MODEL)
        nw = prm['norm_w'][l]
        h = rmsnorm(x, nw[0]) * (1.0 + mod[:, :, 1]) + mod[:, :, 0]
        y, st = mixer_sublayer(h, l, pos0, prev_fn(l), win_keep, prm)
        x = x + mod[:, :, 2] * rmsnorm(y, nw[1])
        h = rmsnorm(x, nw[2]) * (1.0 + mod[:, :, 4]) + mod[:, :, 3]
        f = moe_ffn(h, prm['w_router'][l], prm['b_router'][l], prm['w_exp_gate'][l], prm['w_exp_up'][l],
                    prm['w_exp_down'][l], prm['w_sh_gate'][l], prm['w_sh_up'][l], prm['w_sh_down'][l])
        x = x + mod[:, :, 5] * rmsnorm(f, nw[3])
        for acc, s in zip(outs, st):
            acc.append(s)
    return x, [jnp.stack(a) for a in outs]


def setup_inputs(seed: int = 0) -> dict:
    key = jax.random.key(seed)
    keys = iter(jax.random.split(key, 48))

    def nrm(shape, std):
        return jax.random.normal(next(keys), shape, F32) * std

    n_pages = PAST_LEN // PAGE_SIZE
    n_used = DEC_BATCH * n_pages
    n_pool = n_used + max(1, n_used // 4)
    wbuf = min(WINDOW, PAST_LEN)
    page_table = jax.random.permutation(next(keys), n_pool)[:n_used].reshape(DEC_BATCH, n_pages).astype(jnp.int32)
    return {
        'x_prompt': nrm((BATCH, SEQ, D_MODEL), 1.0),
        'x_sample': nrm((DEC_BATCH, DEC_SEQ, D_MODEL), 1.0),
        'cache_kv': nrm((DEPTH, n_pool, PAGE_SIZE, 4, N_KV, HEAD_DIM), 1.0),
        'state_kv_win': nrm((DEPTH, DEC_BATCH, wbuf, 2, N_KV, HEAD_DIM), 1.0),
        'state_conv_a': nrm((DEPTH, DEC_BATCH, CONV_A - 1, D_A), 1.0),
        'state_conv_b': nrm((DEPTH, DEC_BATCH, CONV_B - 1, D_B), 0.5),
        'state_pool': nrm((DEPTH, DEC_BATCH, POOL_MAX - 1, D_C), 1.0),
        'page_table': page_table,
        'c_prompt': nrm((BATCH, D_MODEL), 1.0),
        'c_sample': nrm((DEC_BATCH, D_MODEL), 1.0),
        'rel_bias': nrm((N_BUCKETS, N_HEADS), 0.5),
        'ada_w': nrm((DEPTH, D_MODEL, 6 * D_MODEL), 0.5 * D_MODEL ** -0.5),
        'ada_b': nrm((DEPTH, 6 * D_MODEL), 0.01),
        'norm_w': 1.0 + nrm((DEPTH, 4, D_MODEL), 0.02),
        'w_in': nrm((DEPTH, D_MODEL, N_IN), D_MODEL ** -0.5),
        'conv_a_w': nrm((DEPTH, CONV_A, D_A), CONV_A ** -0.5),
        'conv_b_w': nrm((DEPTH, CONV_B, D_B), CONV_B ** -0.5),
        'conv_b_b': nrm((DEPTH, D_B), 0.01),
        'ln_b_g': 1.0 + nrm((DEPTH, D_B), 0.02),
        'ln_b_b': nrm((DEPTH, D_B), 0.01),
        'pool_w': nrm((DEPTH, N_POOL_GROUPS, D_CG, D_CG), D_CG ** -0.5),
        'pool_scale': 1.0 + nrm((DEPTH, D_C), 0.1),
        'cmp_pe': nrm((DEPTH, CMP_LEN, 2, HEAD_DIM), 0.1),
        'cmp_w1': nrm((DEPTH, 2, CMP_LEN * HEAD_DIM, HEAD_DIM), (CMP_LEN * HEAD_DIM) ** -0.5),
        'cmp_w2': nrm((DEPTH, 2, HEAD_DIM, HEAD_DIM), HEAD_DIM ** -0.5),
        'w_branch': jnp.concatenate([nrm((DEPTH, D_A, D_MODEL), D_A ** -0.5),
                                     nrm((DEPTH, D_B, D_MODEL), D_B ** -0.5),
                                     nrm((DEPTH, D_C, D_MODEL), D_C ** -0.5),
                                     nrm((DEPTH, D_D, D_MODEL), D_D ** -0.5)], axis=1),
        'w_mgate': nrm((DEPTH, D_MODEL, 4 * D_MODEL), D_MODEL ** -0.5),
        'w_out': nrm((DEPTH, D_MODEL, D_MODEL), D_MODEL ** -0.5),
        'w_router': nrm((DEPTH, D_MODEL, N_EXPERTS), D_MODEL ** -0.5),
        'b_router': nrm((DEPTH, N_EXPERTS), 0.01),
        'w_exp_gate': nrm((DEPTH, N_EXPERTS, D_MODEL, D_EXPERT), D_MODEL ** -0.5),
        'w_exp_up': nrm((DEPTH, N_EXPERTS, D_MODEL, D_EXPERT), D_MODEL ** -0.5),
        'w_exp_down': nrm((DEPTH, N_EXPERTS, D_EXPERT, D_MODEL), D_EXPERT ** -0.5),
        'w_sh_gate': nrm((DEPTH, D_MODEL, D_SHARED), D_MODEL ** -0.5),
        'w_sh_up': nrm((DEPTH, D_MODEL, D_SHARED), D_MODEL ** -0.5),
        'w_sh_down': nrm((DEPTH, D_SHARED, D_MODEL), D_SHARED ** -0.5),
    }


def reference(x_prompt, x_sample, cache_kv, state_kv_win, state_conv_a, state_conv_b, state_pool, page_table,
              c_prompt, c_sample, rel_bias, ada_w, ada_b, norm_w, w_in, conv_a_w, conv_b_w, conv_b_b, ln_b_g, ln_b_b,
              pool_w, pool_scale, cmp_pe, cmp_w1, cmp_w2, w_branch, w_mgate, w_out, w_router, b_router,
              w_exp_gate, w_exp_up, w_exp_down, w_sh_gate, w_sh_up, w_sh_down):
    prm = dict(rel_bias=rel_bias, ada_w=ada_w, ada_b=ada_b, norm_w=norm_w, w_in=w_in, conv_a_w=conv_a_w,
               conv_b_w=conv_b_w, conv_b_b=conv_b_b, ln_b_g=ln_b_g, ln_b_b=ln_b_b, pool_w=pool_w,
               pool_scale=pool_scale, cmp_pe=cmp_pe, cmp_w1=cmp_w1, cmp_w2=cmp_w2, w_branch=w_branch,
               w_mgate=w_mgate, w_out=w_out, w_router=w_router, b_router=b_router, w_exp_gate=w_exp_gate,
               w_exp_up=w_exp_up, w_exp_down=w_exp_down, w_sh_gate=w_sh_gate, w_sh_up=w_sh_up,
               w_sh_down=w_sh_down)
    bp, tp, _ = x_prompt.shape
    bs = x_sample.shape[0]
    dt = x_prompt.dtype
    past_len = page_table.shape[1] * cache_kv.shape[2]

    def prompt_prev(l):
        return (jnp.zeros((bp, 0, 4, N_KV, HEAD_DIM), dt),
                jnp.zeros((bp, WINDOW, 2, N_KV, HEAD_DIM), dt),
                jnp.zeros((bp, CONV_A - 1, D_A), dt),
                jnp.zeros((bp, CONV_B - 1, D_B), dt),
                jnp.zeros((bp, POOL_MAX - 1, D_C), dt))

    def sample_prev(l):
        past = cache_kv[l][page_table].reshape(bs, past_len, 4, N_KV, HEAD_DIM)
        return (past, state_kv_win[l], state_conv_a[l], state_conv_b[l], state_pool[l])

    y_prompt, st_p = run_trunk(x_prompt, c_prompt, 0, prompt_prev, min(WINDOW, tp), prm)
    y_sample, st_s = run_trunk(x_sample, c_sample, past_len, sample_prev, state_kv_win.shape[2], prm)
    kv_p, win_p, ca_p, cb_p, pl_p = st_p
    kv_s, win_s, ca_s, cb_s, pl_s = st_s
    return (y_prompt, y_sample, kv_p, kv_s, win_p, win_s, ca_p, ca_s, cb_p, cb_s, pl_p, pl_s)
```

```python
import functools
import math

import jax
import jax.numpy as jnp
from jax import lax
from jax.experimental import pallas as pl
from jax.experimental.pallas import tpu as pltpu

F32 = jnp.float32
MXU_DTYPE = jnp.bfloat16
VMEM_LIMIT_BYTES = 56 * 1024 * 1024
LANE = 128
SUBLANE = 8

RMS_EPS = 1e-6
LN_EPS = 1e-5
N_KV = 4
HEAD_DIM = 128
POOL_WINDOWS = (2, 4, 8, 16)
CMP_LEN = 32
CMP_STRIDE = 16
SEL_LEN = 64
SEL_TOP = 16
N_LOCAL_SEL = 2
WINDOW = 512
Q_BLOCK = 128
FORCE_SCORE = 1e4
NEG_INF = -1e30
N_BUCKETS = 32
MAX_DISTANCE = 128
TOP_K = 8
MOE_TILE_ROWS = 256


def _pick(n, cands):
    for c in cands:
        if n % c == 0:
            return c
    return n


def _cparams(sem):
    return pltpu.CompilerParams(dimension_semantics=sem, vmem_limit_bytes=VMEM_LIMIT_BYTES)


def _mm_kernel(a_ref, w_ref, o_ref):
    w = w_ref[...].astype(MXU_DTYPE)
    o_ref[...] = jnp.dot(a_ref[...], w, preferred_element_type=F32).astype(o_ref.dtype)


def matmul(a, w, layer=None, out_dtype=F32, tm=None, tn=None):
    M, K = a.shape
    N = w.shape[-1]
    tm = tm or _pick(M, (1024, 512, 256, 128, 64, 32, 16, 8))
    tn = tn or _pick(N, (512, 768, 384, 256, 128))
    if w.ndim == 3:
        w_spec = pl.BlockSpec((None, K, tn), lambda i, j: (layer, 0, j))
    else:
        w_spec = pl.BlockSpec((K, tn), lambda i, j: (0, j))
    return pl.pallas_call(
        _mm_kernel,
        grid=(M // tm, N // tn),
        in_specs=[pl.BlockSpec((tm, K), lambda i, j: (i, 0)), w_spec],
        out_specs=pl.BlockSpec((tm, tn), lambda i, j: (i, j)),
        out_shape=jax.ShapeDtypeStruct((M, N), out_dtype),
        compiler_params=_cparams(("parallel", "parallel")),
    )(a, w)


def _rms(x, w):
    return x * lax.rsqrt(jnp.mean(x * x, axis=-1, keepdims=True) + RMS_EPS) * w


def _rms_mod_kernel(x_ref, nw_ref, sc_ref, sh_ref, h_ref):
    h = _rms(x_ref[...], nw_ref[...]) * sc_ref[0] + sh_ref[0]
    h_ref[...] = h.astype(h_ref.dtype)


def _mod_spec(mod, rows_per_seq, tr):
    Bm, Rm, D = mod.shape
    if Rm == 1:
        per = rows_per_seq // tr
        return pl.BlockSpec((1, 1, D), lambda i: (i // per, 0, 0))
    return pl.BlockSpec((1, Rm, D), lambda i: (i, 0, 0))


def rms_mod(x, nw, sc, sh, rows_per_seq, tr):
    M, D = x.shape
    return pl.pallas_call(
        _rms_mod_kernel,
        grid=(M // tr,),
        in_specs=[pl.BlockSpec((tr, D), lambda i: (i, 0)), pl.BlockSpec((1, D), lambda i: (0, 0)),
                  _mod_spec(sc, rows_per_seq, tr), _mod_spec(sh, rows_per_seq, tr)],
        out_specs=pl.BlockSpec((tr, D), lambda i: (i, 0)),
        out_shape=jax.ShapeDtypeStruct((M, D), MXU_DTYPE),
        compiler_params=_cparams(("parallel",)),
    )(x, nw, sc, sh)


def _resid_kernel(x_ref, y_ref, nwy_ref, gate_ref, nwn_ref, sc_ref, sh_ref, xo_ref, h_ref):
    x = x_ref[...] + gate_ref[0] * _rms(y_ref[...], nwy_ref[...])
    xo_ref[...] = x
    h_ref[...] = (_rms(x, nwn_ref[...]) * sc_ref[0] + sh_ref[0]).astype(h_ref.dtype)


def resid_norm(x, y, nwy, gate, nwn, sc, sh, rows_per_seq, tr):
    M, D = x.shape
    row = pl.BlockSpec((tr, D), lambda i: (i, 0))
    vec = pl.BlockSpec((1, D), lambda i: (0, 0))
    return pl.pallas_call(
        _resid_kernel,
        grid=(M // tr,),
        in_specs=[row, row, vec, _mod_spec(gate, rows_per_seq, tr), vec,
                  _mod_spec(sc, rows_per_seq, tr), _mod_spec(sh, rows_per_seq, tr)],
        out_specs=[row, row],
        out_shape=[jax.ShapeDtypeStruct((M, D), F32), jax.ShapeDtypeStruct((M, D), MXU_DTYPE)],
        compiler_params=_cparams(("parallel",)),
    )(x, y, nwy, gate, nwn, sc, sh)


def _merge_kernel(h_ref, y_ref, wg0, wg1, wg2, wg3, wb_ref, o_ref, *, splits):
    h = h_ref[...]
    acc = None
    for (lo, hi), wg in zip(splits, (wg0, wg1, wg2, wg3)):
        gate = jax.nn.sigmoid(jnp.dot(h, wg[...], preferred_element_type=F32))
        br = jnp.dot(y_ref[:, lo:hi], wb_ref[lo:hi, :], preferred_element_type=F32)
        acc = gate * br if acc is None else acc + gate * br
    o_ref[...] = acc.astype(o_ref.dtype)


def merge(h, ycat, w_mgate, w_branch, layer, splits):
    M, D = h.shape
    C = ycat.shape[1]
    tm = _pick(M, (512, 256, 128, 64, 32, 16, 8))
    tn = _pick(D, (256, 128))
    nj = D // tn
    wg_specs = [pl.BlockSpec((None, D, tn), functools.partial(lambda i, j, k: (layer, 0, k * nj + j), k=k))
                for k in range(4)]
    return pl.pallas_call(
        functools.partial(_merge_kernel, splits=splits),
        grid=(M // tm, nj),
        in_specs=[pl.BlockSpec((tm, D), lambda i, j: (i, 0)), pl.BlockSpec((tm, C), lambda i, j: (i, 0))]
        + wg_specs + [pl.BlockSpec((None, C, tn), lambda i, j: (layer, 0, j))],
        out_specs=pl.BlockSpec((tm, tn), lambda i, j: (i, j)),
        out_shape=jax.ShapeDtypeStruct((M, D), MXU_DTYPE),
        compiler_params=_cparams(("parallel", "parallel")),
    )(h, ycat, w_mgate, w_mgate, w_mgate, w_mgate, w_branch)


def _moe_kernel(te_ref, nu_ref, x_ref, g_ref, wg_ref, wu_ref, wd_ref, o_ref):
    i = pl.program_id(0)

    @pl.when(i < nu_ref[0])
    def _():
        x = x_ref[...]
        gp = jnp.dot(x, wg_ref[...], preferred_element_type=F32)
        up = jnp.dot(x, wu_ref[...], preferred_element_type=F32)
        hid = (gp * jax.nn.sigmoid(gp)) * up * g_ref[...]
        o_ref[...] = jnp.dot(hid.astype(MXU_DTYPE), wd_ref[...], preferred_element_type=F32)

    @pl.when(i >= nu_ref[0])
    def _():
        o_ref[...] = jnp.zeros_like(o_ref)


def moe_grouped(x_perm, gate_perm, tile_expert, n_used, w_eg, w_eu, w_ed, layer, tm):
    P, D = x_perm.shape
    Fe = w_eg.shape[-1]
    grid_spec = pltpu.PrefetchScalarGridSpec(
        num_scalar_prefetch=2,
        grid=(P // tm,),
        in_specs=[pl.BlockSpec((tm, D), lambda i, te, nu: (i, 0)),
                  pl.BlockSpec((tm, 1), lambda i, te, nu: (i, 0)),
                  pl.BlockSpec((None, None, D, Fe), lambda i, te, nu: (layer, te[i], 0, 0)),
                  pl.BlockSpec((None, None, D, Fe), lambda i, te, nu: (layer, te[i], 0, 0)),
                  pl.BlockSpec((None, None, Fe, D), lambda i, te, nu: (layer, te[i], 0, 0))],
        out_specs=pl.BlockSpec((tm, D), lambda i, te, nu: (i, 0)),
    )
    return pl.pallas_call(
        _moe_kernel,
        grid_spec=grid_spec,
        out_shape=jax.ShapeDtypeStruct((P, D), F32),
        compiler_params=_cparams(("arbitrary",)),
    )(tile_expert, n_used, x_perm, gate_perm, w_eg, w_eu, w_ed)


def moe_ffn(h2, prm, layer):
    N, D = h2.shape
    E = prm['b_router'].shape[-1]
    logits = matmul(h2, prm['w_router'], layer)[:, :E]
    aff = jax.nn.sigmoid(logits)
    _, idx = lax.top_k(aff + prm['b_router'][layer].astype(F32), TOP_K)
    top_aff = jnp.take_along_axis(aff, idx, axis=-1)
    wts = top_aff / jnp.sum(top_aff, axis=-1, keepdims=True)

    tm = MOE_TILE_ROWS if N * TOP_K >= E * MOE_TILE_ROWS else 16
    A = N * TOP_K
    e_flat = idx.reshape(A)
    order = jnp.argsort(e_flat, stable=True)
    e_sorted = e_flat[order]
    counts = jnp.zeros((E,), jnp.int32).at[e_flat].add(1)
    tiles = (counts + tm - 1) // tm
    tile_end = jnp.cumsum(tiles)
    tile_off = tile_end - tiles
    n_used = tile_end[-1]
    start = jnp.cumsum(counts) - counts
    rank = jnp.arange(A, dtype=jnp.int32) - start[e_sorted]
    dest_sorted = tile_off[e_sorted] * tm + rank
    n_tiles = -(-A // tm) + E
    P = n_tiles * tm
    src_tok = jnp.zeros((P,), jnp.int32).at[dest_sorted].set((order // TOP_K).astype(jnp.int32))
    gate_p = jnp.zeros((P,), F32).at[dest_sorted].set(wts.reshape(A)[order])
    dest_orig = jnp.zeros((A,), jnp.int32).at[order].set(dest_sorted)
    tile_ids = jnp.arange(n_tiles, dtype=jnp.int32)
    te = jnp.searchsorted(tile_end, tile_ids, side='right').astype(jnp.int32)
    te_last = jnp.searchsorted(tile_end, n_used - 1, side='right').astype(jnp.int32)
    te = jnp.where(tile_ids < n_used, te, te_last)

    x_perm = h2[src_tok]
    y_perm = moe_grouped(x_perm, gate_p[:, None], te, n_used.reshape(1).astype(jnp.int32),
                         prm['w_exp_gate'], prm['w_exp_up'], prm['w_exp_down'], layer, tm)
    routed = jnp.sum(y_perm[dest_orig.reshape(N, TOP_K)], axis=1)

    gu = matmul(h2, prm['w_sh_gu'], layer)
    Fs = gu.shape[1] // 2
    hs = (jax.nn.silu(gu[:, :Fs]) * gu[:, Fs:]).astype(MXU_DTYPE)
    shared = matmul(hs, prm['w_sh_down'], layer)
    return routed + shared


def _softmax_rows(s, mask):
    s = jnp.where(mask, s, NEG_INF)
    m = jnp.max(s, axis=-1, keepdims=True)
    p = jnp.where(mask, jnp.exp(s - m), 0.0)
    return p / jnp.maximum(jnp.sum(p, axis=-1, keepdims=True), 1e-30)


def _dot_nt(a, b):
    return lax.dot_general(a, b, (((1,), (1,)), ((), ())), preferred_element_type=F32)


def _split3(p):
    p1 = p.astype(MXU_DTYPE)
    r1 = p - p1.astype(F32)
    p2 = r1.astype(MXU_DTYPE)
    p3 = (r1 - p2.astype(F32)).astype(MXU_DTYPE)
    return p1, p2, p3


def _nsa_kernel(q_ref, gz_ref, kc_ref, vc_ref, ks_ref, vs_ref, kw_ref, vw_ref, bc_ref, bs_ref, bw_ref, cov_ref,
                o_ref, s_scr, m_scr, l_scr, acc_scr, *, R, bq, ns, n_top):
    qb = pl.program_id(2)
    s0 = pl.multiple_of(qb * bq, bq)
    rows = R * bq
    scale = HEAD_DIM ** -0.5
    q = q_ref[...]
    q_rows = jnp.concatenate([q[:, r * HEAD_DIM:(r + 1) * HEAD_DIM] for r in range(R)], axis=0).astype(MXU_DTYPE)
    i_tok = lax.broadcasted_iota(jnp.int32, (bq, 1), 0)
    i_row = jnp.concatenate([i_tok] * R, axis=0)
    q_pos_row = s0 + i_row

    ncp = kc_ref.shape[0]
    s_c = _dot_nt(q_rows, kc_ref[...]) * scale + bc_ref[...]
    cmp_end = lax.broadcasted_iota(jnp.int32, (1, ncp), 1) * CMP_STRIDE + (CMP_LEN - 1)
    p_c = _softmax_rows(s_c, cmp_end <= q_pos_row)
    o_c = jnp.dot(p_c.astype(MXU_DTYPE), vc_ref[...], preferred_element_type=F32)

    p_sum = p_c[0:bq]
    for r in range(1, R):
        p_sum = p_sum + p_c[r * bq:(r + 1) * bq]
    cov = cov_ref[...]
    p1, p2, p3 = _split3(p_sum)
    imp = (jnp.dot(p3, cov, preferred_element_type=F32) + jnp.dot(p2, cov, preferred_element_type=F32)
           + jnp.dot(p1, cov, preferred_element_type=F32))
    nsp = cov.shape[1]
    blk = lax.broadcasted_iota(jnp.int32, (bq, nsp), 1)
    blk_f = blk.astype(F32)
    q_pos = s0 + i_tok
    cur = jnp.right_shift(q_pos, int(math.log2(SEL_LEN)))
    forced = (blk == 0) | ((blk <= cur) & (blk > cur - N_LOCAL_SEL))
    valid = blk * SEL_LEN <= q_pos
    score = jnp.where(forced, FORCE_SCORE, jnp.where(valid, imp, -1.0))
    score = jnp.where(blk < ns, score, -2.0)
    sel = jnp.zeros((bq, nsp), F32)
    for _ in range(n_top):
        mx = jnp.max(score, axis=-1, keepdims=True)
        first = jnp.min(jnp.where(score == mx, blk_f, float(nsp)), axis=-1, keepdims=True)
        hit = blk_f == first
        sel = jnp.where(hit, 1.0, sel)
        score = jnp.where(hit, -3.0e38, score)
    sel_b = sel.astype(MXU_DTYPE)

    j_lane = lax.broadcasted_iota(jnp.int32, (1, Q_BLOCK), 1)
    e_row = lax.broadcasted_iota(jnp.int32, (nsp, Q_BLOCK), 0)
    e_col = jnp.right_shift(lax.broadcasted_iota(jnp.int32, (nsp, Q_BLOCK), 1), int(math.log2(SEL_LEN)))
    blocks_per_chunk = Q_BLOCK // SEL_LEN
    m_scr[...] = jnp.full(m_scr.shape, NEG_INF, F32)

    def pass1(c, carry):
        k0 = pl.multiple_of(c * Q_BLOCK, Q_BLOCK)
        k = ks_ref[pl.ds(k0, Q_BLOCK), :].astype(MXU_DTYPE)
        d = qb - c
        bias = jnp.where(d == 0, bs_ref[0], jnp.where(d == 1, bs_ref[1], bs_ref[2]))
        s = _dot_nt(q_rows, k) * scale + bias
        expand = (e_row == c * blocks_per_chunk + e_col).astype(MXU_DTYPE)
        picked = jnp.dot(sel_b, expand, preferred_element_type=F32)
        picked = jnp.concatenate([picked] * R, axis=0)
        ok = (picked > 0.5) & ((i_row >= j_lane) | (d > 0))
        s = jnp.where(ok, s, NEG_INF)
        s_scr[c] = s
        m_scr[...] = jnp.maximum(m_scr[...], s)
        return carry

    lax.fori_loop(0, qb + 1, pass1, 0)
    m_s = jnp.max(m_scr[...], axis=-1, keepdims=True)
    l_scr[...] = jnp.zeros(l_scr.shape, F32)
    acc_scr[...] = jnp.zeros(acc_scr.shape, F32)

    def pass2(c, carry):
        k0 = pl.multiple_of(c * Q_BLOCK, Q_BLOCK)
        p = jnp.exp(s_scr[c] - m_s)
        l_scr[...] += p
        acc_scr[...] += jnp.dot(p.astype(MXU_DTYPE), vs_ref[pl.ds(k0, Q_BLOCK), :].astype(MXU_DTYPE),
                                preferred_element_type=F32)
        return carry

    lax.fori_loop(0, qb + 1, pass2, 0)
    o_s = acc_scr[...] / jnp.maximum(jnp.sum(l_scr[...], axis=-1, keepdims=True), 1e-30)

    wb = WINDOW + bq
    kw = kw_ref[pl.ds(s0, wb), :].astype(MXU_DTYPE)
    vw = vw_ref[pl.ds(s0, wb), :].astype(MXU_DTYPE)
    s_w = _dot_nt(q_rows, kw) * scale + bw_ref[...]
    j_w = lax.broadcasted_iota(jnp.int32, (1, wb), 1)
    d_w = i_row - j_w + WINDOW
    m_w = (d_w >= 0) & (d_w < WINDOW) & (s0 - WINDOW + j_w >= 0)
    p_w = _softmax_rows(s_w, m_w)
    o_w = jnp.dot(p_w.astype(MXU_DTYPE), vw, preferred_element_type=F32)

    g = jax.nn.sigmoid(gz_ref[...])
    outs = []
    for r in range(R):
        sl = slice(r * bq, (r + 1) * bq)
        outs.append(g[:, 3 * r:3 * r + 1] * o_c[sl] + g[:, 3 * r + 1:3 * r + 2] * o_s[sl]
                    + g[:, 3 * r + 2:3 * r + 3] * o_w[sl])
    o_ref[...] = jnp.concatenate(outs, axis=1).astype(o_ref.dtype)


def _rel_bucket(dist):
    n = jnp.maximum(dist, 0)
    max_exact = N_BUCKETS // 2
    large = max_exact + (jnp.log(jnp.maximum(n, 1).astype(F32) / max_exact)
                         / math.log(MAX_DISTANCE / max_exact) * (N_BUCKETS - max_exact)).astype(jnp.int32)
    large = jnp.minimum(large, N_BUCKETS - 1)
    return jnp.where(n < max_exact, n, large)


def _bias_tiles(lut, dist):
    t = lut[jnp.clip(dist, 0, MAX_DISTANCE)]
    nd = dist.ndim
    perm = (nd,) + tuple(range(nd - 2)) + (nd + 1, nd - 2, nd - 1)
    t = jnp.transpose(t, perm)
    return t.reshape(t.shape[:-3] + (t.shape[-3] * t.shape[-2], t.shape[-1]))


def _cover(ncp, nsp):
    c0 = jnp.arange(ncp)[:, None] * CMP_STRIDE
    s0 = jnp.arange(nsp)[None, :] * SEL_LEN
    ov = jnp.minimum(c0 + CMP_LEN, s0 + SEL_LEN) - jnp.maximum(c0, s0)
    return (jnp.maximum(ov, 0).astype(F32) / CMP_LEN).astype(MXU_DTYPE)


def _compress(kv_cmp, cmp_pe, cmp_w1, cmp_w2):
    B, L = kv_cmp.shape[:2]
    r = CMP_LEN // CMP_STRIDE
    nch = L // CMP_STRIDE
    nc = nch - r + 1
    chunks = kv_cmp[:, :nch * CMP_STRIDE].reshape(B, nch, CMP_STRIDE, 2, N_KV, HEAD_DIM)
    blocks = jnp.concatenate([chunks[:, m:m + nc] for m in range(r)], axis=2)
    blocks = blocks + cmp_pe[None, None, :, :, None, :]
    flat = jnp.transpose(blocks, (0, 1, 3, 4, 2, 5)).reshape(B, nc, 2, N_KV, CMP_LEN * HEAD_DIM)
    comp = jnp.einsum('bnsge,sed->bnsgd', jax.nn.silu(jnp.einsum('bnsgf,sfe->bnsge', flat, cmp_w1)), cmp_w2)
    return comp[:, :, 0], comp[:, :, 1]


def nsa_prompt(z, B, T, lay, tabs, cmp_pe, cmp_w1, cmp_w2):
    R, G = lay['R'], N_KV
    bq = Q_BLOCK
    nqb = T // bq
    N = B * T
    rows = R * bq
    kv = z[:, lay['okv']:lay['okv'] + 6 * G * HEAD_DIM].reshape(B, T, 6, G, HEAD_DIM)
    kcb, vcb = _compress(kv[:, :, :2], cmp_pe, cmp_w1, cmp_w2)
    nc = kcb.shape[1]
    ncp = tabs['ncp']
    pad = ((0, 0), (0, 0), (0, ncp - nc), (0, 0))
    kcb = jnp.pad(jnp.transpose(kcb, (0, 2, 1, 3)), pad).astype(MXU_DTYPE)
    vcb = jnp.pad(jnp.transpose(vcb, (0, 2, 1, 3)), pad).astype(MXU_DTYPE)
    win = kv[:, :, 4:].reshape(B, T, 2 * G * HEAD_DIM)
    win_pad = jnp.concatenate([jnp.zeros((B, WINDOW, 2 * G * HEAD_DIM), F32), win], axis=1)
    gz = z[:, lay['og']:lay['og'] + 3 * G * R].reshape(N, G, 3 * R)
    gz = jnp.pad(jnp.transpose(gz, (1, 0, 2)), ((0, 0), (0, 0), (0, LANE - 3 * R)))
    ns = -(-T // SEL_LEN)
    nsp = tabs['nsp']
    qcol = lay['oq'] // (R * HEAD_DIM)
    kcol = lay['okv'] // HEAD_DIM
    wb = WINDOW + bq
    kernel = functools.partial(_nsa_kernel, R=R, bq=bq, ns=ns, n_top=min(SEL_TOP, ns))
    return pl.pallas_call(
        kernel,
        grid=(B, G, nqb),
        in_specs=[
            pl.BlockSpec((bq, R * HEAD_DIM), lambda b, g, i: (b * nqb + i, qcol + g)),
            pl.BlockSpec((None, bq, LANE), lambda b, g, i: (g, b * nqb + i, 0)),
            pl.BlockSpec((None, None, ncp, HEAD_DIM), lambda b, g, i: (b, g, 0, 0)),
            pl.BlockSpec((None, None, ncp, HEAD_DIM), lambda b, g, i: (b, g, 0, 0)),
            pl.BlockSpec((T, HEAD_DIM), lambda b, g, i: (b, kcol + 2 * G + g)),
            pl.BlockSpec((T, HEAD_DIM), lambda b, g, i: (b, kcol + 3 * G + g)),
            pl.BlockSpec((None, T + WINDOW, HEAD_DIM), lambda b, g, i: (b, 0, g)),
            pl.BlockSpec((None, T + WINDOW, HEAD_DIM), lambda b, g, i: (b, 0, G + g)),
            pl.BlockSpec((None, None, rows, ncp), lambda b, g, i: (g, i, 0, 0)),
            pl.BlockSpec((None, 3, rows, Q_BLOCK), lambda b, g, i: (g, 0, 0, 0)),
            pl.BlockSpec((None, rows, wb), lambda b, g, i: (g, 0, 0)),
            pl.BlockSpec((ncp, nsp), lambda b, g, i: (0, 0)),
        ],
        out_specs=pl.BlockSpec((bq, R * HEAD_DIM), lambda b, g, i: (b * nqb + i, g)),
        out_shape=jax.ShapeDtypeStruct((N, G * R * HEAD_DIM), F32),
        scratch_shapes=[pltpu.VMEM((nqb, rows, Q_BLOCK), F32), pltpu.VMEM((rows, Q_BLOCK), F32),
                        pltpu.VMEM((rows, Q_BLOCK), F32), pltpu.VMEM((rows, HEAD_DIM), F32)],
        compiler_params=_cparams(("parallel", "parallel", "arbitrary")),
    )(z, gz, kcb, vcb, z, z, win_pad, win_pad, tabs['bias_c'], tabs['bias_s'], tabs['bias_w'], tabs['cover'])


def nsa_prompt_tables(rel_bias, T, R):
    G = N_KV
    bq = Q_BLOCK
    nqb = T // bq
    lut = rel_bias.astype(F32)[_rel_bucket(jnp.arange(MAX_DISTANCE + 1))].reshape(MAX_DISTANCE + 1, G, R)
    nc = T // CMP_STRIDE - CMP_LEN // CMP_STRIDE + 1
    ncp = -(-nc // LANE) * LANE
    ns = -(-T // SEL_LEN)
    nsp = -(-ns // LANE) * LANE
    i = jnp.arange(bq)[:, None]
    t = jnp.arange(T).reshape(nqb, bq, 1)
    cmp_end = (jnp.arange(ncp) * CMP_STRIDE + CMP_LEN - 1)[None, None, :]
    bias_c = _bias_tiles(lut, t - cmp_end)
    j = jnp.arange(Q_BLOCK)[None, :]
    d_s = jnp.stack([i - j, Q_BLOCK + i - j, jnp.full((bq, Q_BLOCK), MAX_DISTANCE)])
    bias_s = _bias_tiles(lut, d_s)
    jw = jnp.arange(WINDOW + bq)[None, :]
    bias_w = _bias_tiles(lut, i - jw + WINDOW)
    return dict(bias_c=bias_c, bias_s=bias_s, bias_w=bias_w, cover=_cover(ncp, nsp), ncp=ncp, nsp=nsp)


def _masked_softmax(s, mask, axis):
    s = jnp.where(mask, s, NEG_INF)
    m = jnp.max(s, axis=axis, keepdims=True)
    p = jnp.where(mask, jnp.exp(s - m), 0.0)
    return p / jnp.maximum(jnp.sum(p, axis=axis, keepdims=True), 1e-30)


def nsa_decode(q, g, kv_full, kw_rows, kw_pos0, q_pos0, rel_bias, cmp_pe, cmp_w1, cmp_w2):
    B, T, G, R, Dh = q.shape
    L = kv_full.shape[1]
    kcb, vcb = _compress(kv_full[:, :, :2], cmp_pe, cmp_w1, cmp_w2)
    nc = kcb.shape[1]
    cmp_end = jnp.arange(nc) * CMP_STRIDE + (CMP_LEN - 1)
    ns = -(-L // SEL_LEN)
    sel = jnp.pad(kv_full[:, :, 2:], ((0, 0), (0, ns * SEL_LEN - L), (0, 0), (0, 0), (0, 0)))
    sel = jnp.transpose(sel.reshape(B, ns, SEL_LEN, 2, G, Dh), (3, 0, 4, 1, 2, 5))
    ks_b, vs_b = sel[0], sel[1]
    c0 = jnp.arange(nc)[:, None] * CMP_STRIDE
    s0 = jnp.arange(ns)[None, :] * SEL_LEN
    cover = jnp.maximum(jnp.minimum(c0 + CMP_LEN, s0 + SEL_LEN) - jnp.maximum(c0, s0), 0).astype(F32) / CMP_LEN
    q_pos = q_pos0 + jnp.arange(T)
    W = kw_rows.shape[1]
    kw, vw = kw_rows[:, :, 0], kw_rows[:, :, 1]
    kw_pos = kw_pos0 + jnp.arange(W)
    scale = Dh ** -0.5
    table = rel_bias.astype(F32).reshape(N_BUCKETS, G, R)

    def bias_2d(dist):
        return jnp.transpose(table[_rel_bucket(dist)], (2, 3, 0, 1))

    d_c = q_pos[:, None] - cmp_end[None, :]
    s_c = jnp.einsum('btgrd,bngd->bgrtn', q, kcb, preferred_element_type=F32) * scale + bias_2d(d_c)
    p_c = _masked_softmax(s_c, cmp_end[None, :] <= q_pos[:, None], -1)
    o_c = jnp.einsum('bgrtn,bngd->btgrd', p_c, vcb)
    imp = jnp.einsum('bgrtn,ns->bgts', p_c, cover, precision=lax.Precision.HIGHEST)
    blk = jnp.arange(ns)[None, :]
    cur = (q_pos // SEL_LEN)[:, None]
    forced = (blk == 0) | ((blk <= cur) & (blk > cur - N_LOCAL_SEL))
    valid = blk * SEL_LEN <= q_pos[:, None]
    score = jnp.where(forced, FORCE_SCORE, jnp.where(valid, imp, -1.0))
    _, idx = lax.top_k(score, min(SEL_TOP, ns))
    bi = jnp.arange(B)[:, None, None, None]
    gi = jnp.arange(G)[None, :, None, None]
    ks = ks_b[bi, gi, idx]
    vs = vs_b[bi, gi, idx]
    kpos = idx[..., None] * SEL_LEN + jnp.arange(SEL_LEN)
    d_s = q_pos[None, None, :, None, None] - kpos
    b_s = table[_rel_bucket(d_s), jnp.arange(G)[None, :, None, None, None]]
    s_s = jnp.einsum('btgrd,bgtnkd->bgtnkr', q, ks, preferred_element_type=F32) * scale + b_s
    M = idx.shape[-1] * SEL_LEN
    p_s = _masked_softmax(s_s.reshape(B, G, T, M, R), (d_s >= 0).reshape(B, G, T, M, 1), -2)
    o_s = jnp.einsum('bgtmr,bgtmd->btgrd', p_s, vs.reshape(B, G, T, M, Dh))
    d_w = q_pos[:, None] - kw_pos[None, :]
    m_w = (d_w >= 0) & (d_w < WINDOW) & (kw_pos[None, :] >= 0)
    s_w = jnp.einsum('btgrd,bkgd->bgrtk', q, kw, preferred_element_type=F32) * scale + bias_2d(d_w)
    p_w = _masked_softmax(s_w, m_w, -1)
    o_w = jnp.einsum('bgrtk,bkgd->btgrd', p_w, vw)
    out = g[..., 0:1] * o_c + g[..., 1:2] * o_s + g[..., 2:3] * o_w
    return out.reshape(B, T, G * R * Dh)


def _dwconv(ext, w, T):
    acc = None
    for k in range(w.shape[0]):
        term = w[k][None, None, :] * ext[:, k:k + T]
        acc = term if acc is None else acc + term
    return acc


def _layernorm(x, g, b):
    mu = jnp.mean(x, axis=-1, keepdims=True)
    var = jnp.mean(jnp.square(x - mu), axis=-1, keepdims=True)
    return (x - mu) * lax.rsqrt(var + LN_EPS) * g + b


def _pool(ext, u, pos0, w_grp, scale):
    B, T, C = u.shape
    P = ext.shape[1] - T
    ngrp = len(POOL_WINDOWS)
    cg = C // ngrp
    pos = pos0 + jnp.arange(T)
    outs = []
    for gi, w in enumerate(POOL_WINDOWS):
        sl = slice(gi * cg, (gi + 1) * cg)
        win_sum = None
        for k in range(w):
            term = ext[:, P - k:P - k + T, sl]
            win_sum = term if win_sum is None else win_sum + term
        cnt = jnp.minimum(w, pos + 1).astype(F32)
        outs.append(win_sum / cnt[None, :, None])
    pooled = jnp.concatenate(outs, axis=-1) - u
    mixed = jnp.einsum('btgc,gce->btge', pooled.reshape(B, T, ngrp, cg), w_grp).reshape(B, T, C)
    return mixed * scale


def _layout(prm, D):
    DA = prm['conv_a_w'].shape[-1]
    DB = prm['conv_b_w'].shape[-1]
    DC = prm['pool_scale'].shape[-1]
    NH = prm['rel_bias'].shape[1]
    R = NH // N_KV
    DD = NH * HEAD_DIM
    ob = 3 * DA
    oc = ob + 2 * DB
    oq = oc + DC
    okv = oq + DD
    og = okv + 6 * N_KV * HEAD_DIM
    assert oq % (R * HEAD_DIM) == 0 and okv % HEAD_DIM == 0
    return dict(DA=DA, DB=DB, DC=DC, R=R, DD=DD, ob=ob, oc=oc, oq=oq, okv=okv, og=og, n_in=og + 3 * NH)


def _mixer(z, B, T, l, pos0, prev, win_keep, prm, lay, tabs):
    kv_past, win_prev, ca_prev, cb_prev, pool_prev = prev
    DA, DB, DC, R = lay['DA'], lay['DB'], lay['DC'], lay['R']
    z3 = z.reshape(B, T, z.shape[-1])
    a_b, a_c, a_h = z3[..., :DA], z3[..., DA:2 * DA], z3[..., 2 * DA:3 * DA]
    ext_a = jnp.concatenate([ca_prev, a_c * a_h], axis=1)
    y_a = a_b * _dwconv(ext_a, prm['conv_a_w'][l], T)
    ob = lay['ob']
    b_lin, b_gate = z3[..., ob:ob + DB], z3[..., ob + DB:ob + 2 * DB]
    ext_b = jnp.concatenate([cb_prev, b_lin * jax.nn.sigmoid(b_gate)], axis=1)
    cb = _dwconv(ext_b, prm['conv_b_w'][l], T) + prm['conv_b_b'][l]
    y_b = jax.nn.silu(_layernorm(cb, prm['ln_b_g'][l], prm['ln_b_b'][l]))
    zc = z3[..., lay['oc']:lay['oc'] + DC]
    ext_c = jnp.concatenate([pool_prev, zc], axis=1)
    y_c = _pool(ext_c, zc, pos0, prm['pool_w'][l], prm['pool_scale'][l])
    kv = z3[..., lay['okv']:lay['okv'] + 6 * N_KV * HEAD_DIM].reshape(B, T, 6, N_KV, HEAD_DIM)
    win_rows = jnp.concatenate([win_prev, kv[:, :, 4:]], axis=1)
    if kv_past is None:
        y_d = nsa_prompt(z, B, T, lay, tabs, prm['cmp_pe'][l], prm['cmp_w1'][l], prm['cmp_w2'][l])
        y_d = y_d.reshape(B, T, -1)
    else:
        q = z3[..., lay['oq']:lay['oq'] + lay['DD']].reshape(B, T, N_KV, R, HEAD_DIM)
        g = jax.nn.sigmoid(z3[..., lay['og']:lay['og'] + 3 * N_KV * R].reshape(B, T, N_KV, R, 3))
        kv_full = jnp.concatenate([kv_past, kv[:, :, :4]], axis=1)
        y_d = nsa_decode(q, g, kv_full, win_rows, pos0 - win_prev.shape[1], pos0, prm['rel_bias'],
                         prm['cmp_pe'][l], prm['cmp_w1'][l], prm['cmp_w2'][l])
    ycat = jnp.concatenate([y_a, y_b, y_c, y_d], axis=-1).astype(MXU_DTYPE).reshape(B * T, -1)
    state = (kv[:, :, :4], win_rows[:, -win_keep:], ext_a[:, -ca_prev.shape[1]:],
             ext_b[:, -cb_prev.shape[1]:], ext_c[:, -pool_prev.shape[1]:])
    return ycat, state


def _run_trunk(x3, mods, pos0, prev_fn, win_keep, prm, lay, tabs):
    B, T, D = x3.shape
    N = B * T
    depth = prm['norm_w'].shape[0]
    x = x3.reshape(N, D)
    tr = _pick(N, (256, 128, 64, 32, 16, 8))
    if T % tr == 0:
        per_seq = lambda m: m[:, None, :]
    else:
        assert tr == N
        per_seq = lambda m: jnp.repeat(m, T, axis=0)[None]
    splits = []
    acc = 0
    for wdt in (lay['DA'], lay['DB'], lay['DC'], lay['DD']):
        splits.append((acc, acc + wdt))
        acc += wdt
    outs = ([], [], [], [], [])
    nw = prm['norm_w']
    mod = mods[0]
    h = rms_mod(x, nw[0, 0][None], per_seq(1.0 + mod[:, 1]), per_seq(mod[:, 0]), T, tr)
    for l in range(depth):
        mod = mods[l]
        z = matmul(h, prm['w_in'], l)
        ycat, st = _mixer(z, B, T, l, pos0, prev_fn(l), win_keep, prm, lay, tabs)
        merged = merge(h, ycat, prm['w_mgate'], prm['w_branch'], l, tuple(splits))
        y = matmul(merged, prm['w_out'], l)
        x, h2 = resid_norm(x, y, nw[l, 1][None], per_seq(mod[:, 2]), nw[l, 2][None],
                           per_seq(1.0 + mod[:, 4]), per_seq(mod[:, 3]), T, tr)
        f = moe_ffn(h2, prm, l)
        ln = min(l + 1, depth - 1)
        modn = mods[ln]
        x, h = resid_norm(x, f, nw[l, 3][None], per_seq(mod[:, 5]), nw[ln, 0][None],
                          per_seq(1.0 + modn[:, 1]), per_seq(modn[:, 0]), T, tr)
        for a, s in zip(outs, st):
            a.append(s)
    return x.reshape(B, T, D), [jnp.stack(a) for a in outs]


def kernel(x_prompt, x_sample, cache_kv, state_kv_win, state_conv_a, state_conv_b, state_pool, page_table,
           c_prompt, c_sample, rel_bias, ada_w, ada_b, norm_w, w_in, conv_a_w, conv_b_w, conv_b_b, ln_b_g, ln_b_b,
           pool_w, pool_scale, cmp_pe, cmp_w1, cmp_w2, w_branch, w_mgate, w_out, w_router, b_router,
           w_exp_gate, w_exp_up, w_exp_down, w_sh_gate, w_sh_up, w_sh_down):
    bp, tp, D = x_prompt.shape
    bs, ts, _ = x_sample.shape
    depth = norm_w.shape[0]
    dt = x_prompt.dtype
    past_len = page_table.shape[1] * cache_kv.shape[2]
    E = b_router.shape[-1]

    prm = dict(rel_bias=rel_bias, norm_w=norm_w, conv_a_w=conv_a_w, conv_b_w=conv_b_w, conv_b_b=conv_b_b,
               ln_b_g=ln_b_g, ln_b_b=ln_b_b, pool_w=pool_w, pool_scale=pool_scale, cmp_pe=cmp_pe, cmp_w1=cmp_w1,
               cmp_w2=cmp_w2, b_router=b_router)
    lay = _layout(prm, D)
    n_in = lay['n_in']
    nz = lay['og'] + 2 * LANE
    prm['w_in'] = jnp.pad(w_in, ((0, 0), (0, 0), (0, nz - n_in))).astype(MXU_DTYPE)
    prm['w_mgate'] = w_mgate.astype(MXU_DTYPE)
    prm['w_branch'] = w_branch.astype(MXU_DTYPE)
    prm['w_out'] = w_out.astype(MXU_DTYPE)
    prm['w_router'] = jnp.pad(w_router, ((0, 0), (0, 0), (0, LANE - E))).astype(MXU_DTYPE)
    prm['w_exp_gate'] = w_exp_gate.astype(MXU_DTYPE)
    prm['w_exp_up'] = w_exp_up.astype(MXU_DTYPE)
    prm['w_exp_down'] = w_exp_down.astype(MXU_DTYPE)
    prm['w_sh_gu'] = jnp.concatenate([w_sh_gate, w_sh_up], axis=-1).astype(MXU_DTYPE)
    prm['w_sh_down'] = w_sh_down.astype(MXU_DTYPE)

    nb = bp + bs
    nbp = -(-nb // 16) * 16
    cs = jnp.pad(jax.nn.silu(jnp.concatenate([c_prompt, c_sample], axis=0)), ((0, nbp - nb), (0, 0)))
    cs = cs.astype(MXU_DTYPE)
    mods_p, mods_s = [], []
    for l in range(depth):
        m = (matmul(cs, ada_w, l, tn=_pick(6 * D, (512, 256, 128)))[:nb] + ada_b[l]).reshape(nb, 6, D)
        mods_p.append(m[:bp])
        mods_s.append(m[bp:])

    tabs = nsa_prompt_tables(rel_bias, tp, lay['R'])

    def prompt_prev(l):
        return (None,
                jnp.zeros((bp, WINDOW, 2, N_KV, HEAD_DIM), dt),
                jnp.zeros((bp, conv_a_w.shape[1] - 1, lay['DA']), dt),
                jnp.zeros((bp, conv_b_w.shape[1] - 1, lay['DB']), dt),
                jnp.zeros((bp, state_pool.shape[2], lay['DC']), dt))

    def sample_prev(l):
        past = cache_kv[l][page_table].reshape(bs, past_len, 4, N_KV, HEAD_DIM)
        return (past, state_kv_win[l], state_conv_a[l], state_conv_b[l], state_pool[l])

    y_prompt, st_p = _run_trunk(x_prompt, mods_p, 0, prompt_prev, min(WINDOW, tp), prm, lay, tabs)
    y_sample, st_s = _run_trunk(x_sample, mods_s, past_len, sample_prev, state_kv_win.shape[2], prm, lay, None)
    kv_p, win_p, ca_p, cb_p, pl_p = st_p
    kv_s, win_s, ca_s, cb_s, pl_s = st_s
    return (y_prompt, y_sample, kv_p, kv_s, win_p, win_s, ca_p, ca_s, cb_p, cb_s, pl_p, pl_s)
```

```python
import functools
import math

import jax
import jax.numpy as jnp
from jax import lax
from jax.experimental import pallas as pl
from jax.experimental.pallas import tpu as pltpu

F32 = jnp.float32
MXU_DTYPE = jnp.bfloat16
VMEM_LIMIT_BYTES = 56 * 1024 * 1024
LANE = 128
SUBLANE = 8

RMS_EPS = 1e-6
LN_EPS = 1e-5
N_KV = 4
HEAD_DIM = 128
POOL_WINDOWS = (2, 4, 8, 16)
CMP_LEN = 32
CMP_STRIDE = 16
SEL_LEN = 64
SEL_TOP = 16
N_LOCAL_SEL = 2
WINDOW = 512
Q_BLOCK = 128
FORCE_SCORE = 1e4
NEG_INF = -1e30
N_BUCKETS = 32
MAX_DISTANCE = 128
TOP_K = 8
MOE_TILE_ROWS = 256


def _pick(n, cands):
    for c in cands:
        if n % c == 0:
            return c
    return n


def _cparams(sem):
    return pltpu.CompilerParams(dimension_semantics=sem, vmem_limit_bytes=VMEM_LIMIT_BYTES)


def _mm_kernel(a_ref, w_ref, o_ref):
    w = w_ref[...].astype(MXU_DTYPE)
    o_ref[...] = jnp.dot(a_ref[...], w, preferred_element_type=F32).astype(o_ref.dtype)


def matmul(a, w, layer=None, out_dtype=F32, tm=None, tn=None):
    M, K = a.shape
    N = w.shape[-1]
    tm = tm or _pick(M, (1024, 512, 256, 128, 64, 32, 16, 8))
    tn = tn or _pick(N, (512, 768, 384, 256, 128))
    if w.ndim == 3:
        w_spec = pl.BlockSpec((None, K, tn), lambda i, j: (layer, 0, j))
    else:
        w_spec = pl.BlockSpec((K, tn), lambda i, j: (0, j))
    return pl.pallas_call(
        _mm_kernel,
        grid=(M // tm, N // tn),
        in_specs=[pl.BlockSpec((tm, K), lambda i, j: (i, 0)), w_spec],
        out_specs=pl.BlockSpec((tm, tn), lambda i, j: (i, j)),
        out_shape=jax.ShapeDtypeStruct((M, N), out_dtype),
        compiler_params=_cparams(("parallel", "parallel")),
    )(a, w)


def _rms(x, w):
    return x * lax.rsqrt(jnp.mean(x * x, axis=-1, keepdims=True) + RMS_EPS) * w


def _rms_mod_kernel(x_ref, nw_ref, sc_ref, sh_ref, h_ref):
    h = _rms(x_ref[...], nw_ref[...]) * sc_ref[0] + sh_ref[0]
    h_ref[...] = h.astype(h_ref.dtype)


def _mod_spec(mod, rows_per_seq, tr):
    Bm, Rm, D = mod.shape
    if Rm == 1:
        per = rows_per_seq // tr
        return pl.BlockSpec((1, 1, D), lambda i: (i // per, 0, 0))
    return pl.BlockSpec((1, Rm, D), lambda i: (i, 0, 0))


def rms_mod(x, nw, sc, sh, rows_per_seq, tr):
    M, D = x.shape
    return pl.pallas_call(
        _rms_mod_kernel,
        grid=(M // tr,),
        in_specs=[pl.BlockSpec((tr, D), lambda i: (i, 0)), pl.BlockSpec((1, D), lambda i: (0, 0)),
                  _mod_spec(sc, rows_per_seq, tr), _mod_spec(sh, rows_per_seq, tr)],
        out_specs=pl.BlockSpec((tr, D), lambda i: (i, 0)),
        out_shape=jax.ShapeDtypeStruct((M, D), MXU_DTYPE),
        compiler_params=_cparams(("parallel",)),
    )(x, nw, sc, sh)


def _resid_kernel(x_ref, y_ref, nwy_ref, gate_ref, nwn_ref, sc_ref, sh_ref, xo_ref, h_ref):
    x = x_ref[...] + gate_ref[0] * _rms(y_ref[...], nwy_ref[...])
    xo_ref[...] = x
    h_ref[...] = (_rms(x, nwn_ref[...]) * sc_ref[0] + sh_ref[0]).astype(h_ref.dtype)


def resid_norm(x, y, nwy, gate, nwn, sc, sh, rows_per_seq, tr):
    M, D = x.shape
    row = pl.BlockSpec((tr, D), lambda i: (i, 0))
    vec = pl.BlockSpec((1, D), lambda i: (0, 0))
    return pl.pallas_call(
        _resid_kernel,
        grid=(M // tr,),
        in_specs=[row, row, vec, _mod_spec(gate, rows_per_seq, tr), vec,
                  _mod_spec(sc, rows_per_seq, tr), _mod_spec(sh, rows_per_seq, tr)],
        out_specs=[row, row],
        out_shape=[jax.ShapeDtypeStruct((M, D), F32), jax.ShapeDtypeStruct((M, D), MXU_DTYPE)],
        compiler_params=_cparams(("parallel",)),
    )(x, y, nwy, gate, nwn, sc, sh)


def _merge_kernel(h_ref, y1_ref, y2_ref, wg0, wg1, wg2, wg3, wb_ref, o_ref, *, splits):
    h = h_ref[...]
    c1 = y1_ref.shape[1]
    acc = None
    for (lo, hi), wg in zip(splits, (wg0, wg1, wg2, wg3)):
        gate = jax.nn.sigmoid(jnp.dot(h, wg[...], preferred_element_type=F32))
        y = y1_ref[:, lo:hi] if hi <= c1 else y2_ref[:, lo - c1:hi - c1]
        br = jnp.dot(y, wb_ref[lo:hi, :], preferred_element_type=F32)
        acc = gate * br if acc is None else acc + gate * br
    o_ref[...] = acc.astype(o_ref.dtype)


def merge(h, y1, y2, w_mgate, w_branch, layer, splits):
    M, D = h.shape
    C1, C2 = y1.shape[1], y2.shape[1]
    tm = _pick(M, (512, 256, 128, 64, 32, 16, 8))
    tn = _pick(D, (256, 128))
    nj = D // tn
    wg_specs = [pl.BlockSpec((None, D, tn), functools.partial(lambda i, j, k: (layer, 0, k * nj + j), k=k))
                for k in range(4)]
    return pl.pallas_call(
        functools.partial(_merge_kernel, splits=splits),
        grid=(M // tm, nj),
        in_specs=[pl.BlockSpec((tm, D), lambda i, j: (i, 0)), pl.BlockSpec((tm, C1), lambda i, j: (i, 0)),
                  pl.BlockSpec((tm, C2), lambda i, j: (i, 0))]
        + wg_specs + [pl.BlockSpec((None, C1 + C2, tn), lambda i, j: (layer, 0, j))],
        out_specs=pl.BlockSpec((tm, tn), lambda i, j: (i, j)),
        out_shape=jax.ShapeDtypeStruct((M, D), MXU_DTYPE),
        compiler_params=_cparams(("parallel", "parallel")),
    )(h, y1, y2, w_mgate, w_mgate, w_mgate, w_mgate, w_branch)


def _moe_kernel(te_ref, nu_ref, x_ref, g_ref, wg_ref, wu_ref, wd_ref, o_ref):
    i = pl.program_id(0)

    @pl.when(i < nu_ref[0])
    def _():
        x = x_ref[...]
        gp = jnp.dot(x, wg_ref[...], preferred_element_type=F32)
        up = jnp.dot(x, wu_ref[...], preferred_element_type=F32)
        hid = (gp * jax.nn.sigmoid(gp)) * up * g_ref[...]
        o_ref[...] = jnp.dot(hid.astype(MXU_DTYPE), wd_ref[...], preferred_element_type=F32)

    @pl.when(i >= nu_ref[0])
    def _():
        o_ref[...] = jnp.zeros_like(o_ref)


def moe_grouped(x_perm, gate_perm, tile_expert, n_used, w_eg, w_eu, w_ed, layer, tm):
    P, D = x_perm.shape
    Fe = w_eg.shape[-1]
    grid_spec = pltpu.PrefetchScalarGridSpec(
        num_scalar_prefetch=2,
        grid=(P // tm,),
        in_specs=[pl.BlockSpec((tm, D), lambda i, te, nu: (i, 0)),
                  pl.BlockSpec((tm, 1), lambda i, te, nu: (i, 0)),
                  pl.BlockSpec((None, None, D, Fe), lambda i, te, nu: (layer, te[i], 0, 0)),
                  pl.BlockSpec((None, None, D, Fe), lambda i, te, nu: (layer, te[i], 0, 0)),
                  pl.BlockSpec((None, None, Fe, D), lambda i, te, nu: (layer, te[i], 0, 0))],
        out_specs=pl.BlockSpec((tm, D), lambda i, te, nu: (i, 0)),
    )
    return pl.pallas_call(
        _moe_kernel,
        grid_spec=grid_spec,
        out_shape=jax.ShapeDtypeStruct((P, D), F32),
        compiler_params=_cparams(("arbitrary",)),
    )(tile_expert, n_used, x_perm, gate_perm, w_eg, w_eu, w_ed)


def moe_ffn(h2, prm, layer):
    N, D = h2.shape
    E = prm['b_router'].shape[-1]
    logits = matmul(h2, prm['w_router'], layer)[:, :E]
    aff = jax.nn.sigmoid(logits)
    _, idx = lax.top_k(aff + prm['b_router'][layer].astype(F32), TOP_K)
    top_aff = jnp.take_along_axis(aff, idx, axis=-1)
    wts = top_aff / jnp.sum(top_aff, axis=-1, keepdims=True)

    tm = MOE_TILE_ROWS if N * TOP_K >= E * MOE_TILE_ROWS else 16
    A = N * TOP_K
    e_flat = idx.reshape(A)
    order = jnp.argsort(e_flat, stable=True)
    e_sorted = e_flat[order]
    counts = jnp.zeros((E,), jnp.int32).at[e_flat].add(1)
    tiles = (counts + tm - 1) // tm
    tile_end = jnp.cumsum(tiles)
    tile_off = tile_end - tiles
    n_used = tile_end[-1]
    start = jnp.cumsum(counts) - counts
    rank = jnp.arange(A, dtype=jnp.int32) - start[e_sorted]
    dest_sorted = tile_off[e_sorted] * tm + rank
    n_tiles = -(-A // tm) + E
    P = n_tiles * tm
    src_tok = jnp.zeros((P,), jnp.int32).at[dest_sorted].set((order // TOP_K).astype(jnp.int32))
    gate_p = jnp.zeros((P,), F32).at[dest_sorted].set(wts.reshape(A)[order])
    dest_orig = jnp.zeros((A,), jnp.int32).at[order].set(dest_sorted)
    tile_ids = jnp.arange(n_tiles, dtype=jnp.int32)
    te = jnp.sum(tile_ids[:, None] >= tile_end[None, :], axis=1).astype(jnp.int32)
    te_last = jnp.sum(n_used - 1 >= tile_end).astype(jnp.int32)
    te = jnp.where(tile_ids < n_used, te, te_last)

    x_perm = h2[src_tok]
    y_perm = moe_grouped(x_perm, gate_p[:, None], te, n_used.reshape(1).astype(jnp.int32),
                         prm['w_exp_gate'], prm['w_exp_up'], prm['w_exp_down'], layer, tm)
    routed = jnp.sum(y_perm[dest_orig.reshape(N, TOP_K)], axis=1)

    gu = matmul(h2, prm['w_sh_gu'], layer)
    Fs = gu.shape[1] // 2
    hs = (jax.nn.silu(gu[:, :Fs]) * gu[:, Fs:]).astype(MXU_DTYPE)
    shared = matmul(hs, prm['w_sh_down'], layer)
    return routed + shared


def _softmax_rows(s, mask):
    s = jnp.where(mask, s, NEG_INF)
    m = jnp.max(s, axis=-1, keepdims=True)
    p = jnp.where(mask, jnp.exp(s - m), 0.0)
    return p / jnp.maximum(jnp.sum(p, axis=-1, keepdims=True), 1e-30)


def _dot_nt(a, b):
    return lax.dot_general(a, b, (((1,), (1,)), ((), ())), preferred_element_type=F32)


def _nsa_kernel(q_ref, gz_ref, kc_ref, vc_ref, ks_ref, vs_ref, kw_ref, vw_ref, bc_ref, bs_ref, bw_ref, covt_ref,
                o_ref, s_scr, m_scr, l_scr, acc_scr, *, R, bq, ns, n_top):
    qb = pl.program_id(2)
    s0 = pl.multiple_of(qb * bq, bq)
    rows = R * bq
    scale = HEAD_DIM ** -0.5
    q = q_ref[...]
    q_rows = jnp.concatenate([q[:, r * HEAD_DIM:(r + 1) * HEAD_DIM] for r in range(R)], axis=0).astype(MXU_DTYPE)
    i_tok = lax.broadcasted_iota(jnp.int32, (bq, 1), 0)
    i_row = jnp.concatenate([i_tok] * R, axis=0)
    q_pos_row = s0 + i_row

    ncp = kc_ref.shape[0]
    s_c = _dot_nt(q_rows, kc_ref[...]) * scale + bc_ref[...]
    cmp_end = lax.broadcasted_iota(jnp.int32, (1, ncp), 1) * CMP_STRIDE + (CMP_LEN - 1)
    p_c = _softmax_rows(s_c, cmp_end <= q_pos_row).astype(MXU_DTYPE)
    o_c = jnp.dot(p_c, vc_ref[...], preferred_element_type=F32)

    wb = WINDOW + bq
    kw = kw_ref[pl.ds(s0, wb), :].astype(MXU_DTYPE)
    vw = vw_ref[pl.ds(s0, wb), :].astype(MXU_DTYPE)
    s_w = _dot_nt(q_rows, kw) * scale + bw_ref[...]
    j_w = lax.broadcasted_iota(jnp.int32, (1, wb), 1)
    d_w = i_row - j_w + WINDOW
    m_w = (d_w >= 0) & (d_w < WINDOW) & (s0 - WINDOW + j_w >= 0)
    p_w = _softmax_rows(s_w, m_w)
    o_w = jnp.dot(p_w.astype(MXU_DTYPE), vw, preferred_element_type=F32)

    covt = covt_ref[...]
    imp_rows = _dot_nt(covt, p_c)
    imp = imp_rows[:, 0:bq]
    for r in range(1, R):
        imp = imp + imp_rows[:, r * bq:(r + 1) * bq]
    nsr = covt.shape[0]
    blk = lax.broadcasted_iota(jnp.int32, (nsr, bq), 0)
    blk_f = blk.astype(F32)
    q_pos = s0 + lax.broadcasted_iota(jnp.int32, (1, bq), 1)
    cur = jnp.right_shift(q_pos, int(math.log2(SEL_LEN)))
    forced = (blk == 0) | ((blk <= cur) & (blk > cur - N_LOCAL_SEL))
    valid = blk * SEL_LEN <= q_pos
    score = jnp.where(forced, FORCE_SCORE, jnp.where(valid, imp, -1.0))
    score = jnp.where(blk < ns, score, -2.0)
    sel = jnp.zeros((nsr, bq), F32)
    for _ in range(n_top):
        mx = jnp.max(score, axis=0, keepdims=True)
        first = jnp.min(jnp.where(score == mx, blk_f, float(nsr)), axis=0, keepdims=True)
        hit = blk_f == first
        sel = jnp.where(hit, 1.0, sel)
        score = jnp.where(hit, -3.0e38, score)

    j_lane = lax.broadcasted_iota(jnp.int32, (1, Q_BLOCK), 1)
    e_row = lax.broadcasted_iota(jnp.int32, (nsr, Q_BLOCK), 0)
    e_col = jnp.right_shift(lax.broadcasted_iota(jnp.int32, (nsr, Q_BLOCK), 1), int(math.log2(SEL_LEN)))
    blocks_per_chunk = Q_BLOCK // SEL_LEN
    m_scr[...] = jnp.full(m_scr.shape, NEG_INF, F32)

    def pass1(c, carry):
        k0 = pl.multiple_of(c * Q_BLOCK, Q_BLOCK)
        k = ks_ref[pl.ds(k0, Q_BLOCK), :].astype(MXU_DTYPE)
        d = qb - c
        bias = jnp.where(d == 0, bs_ref[0], jnp.where(d == 1, bs_ref[1], bs_ref[2]))
        s = _dot_nt(q_rows, k) * scale + bias
        expand = (e_row == c * blocks_per_chunk + e_col).astype(F32)
        picked = lax.dot_general(sel, expand, (((0,), (0,)), ((), ())), preferred_element_type=F32)
        picked = jnp.concatenate([picked] * R, axis=0)
        ok = (picked > 0.5) & ((i_row >= j_lane) | (d > 0))
        s = jnp.where(ok, s, NEG_INF)
        s_scr[c] = s
        m_scr[...] = jnp.maximum(m_scr[...], s)
        return carry

    lax.fori_loop(0, qb + 1, pass1, 0)
    m_s = jnp.max(m_scr[...], axis=-1, keepdims=True)
    l_scr[...] = jnp.zeros(l_scr.shape, F32)
    acc_scr[...] = jnp.zeros(acc_scr.shape, F32)

    def pass2(c, carry):
        p = jnp.exp(s_scr[c] - m_s)
        l_scr[...] += p
        s_scr[c] = p
        return carry

    lax.fori_loop(0, qb + 1, pass2, 0)
    inv_l = 1.0 / jnp.maximum(jnp.sum(l_scr[...], axis=-1, keepdims=True), 1e-30)

    def pass3(c, carry):
        k0 = pl.multiple_of(c * Q_BLOCK, Q_BLOCK)
        p = (s_scr[c] * inv_l).astype(MXU_DTYPE)
        acc_scr[...] += jnp.dot(p, vs_ref[pl.ds(k0, Q_BLOCK), :].astype(MXU_DTYPE), preferred_element_type=F32)
        return carry

    lax.fori_loop(0, qb + 1, pass3, 0)
    o_s = acc_scr[...]

    g = jax.nn.sigmoid(gz_ref[...])
    outs = []
    for r in range(R):
        sl = slice(r * bq, (r + 1) * bq)
        outs.append(g[:, 3 * r:3 * r + 1] * o_c[sl] + g[:, 3 * r + 1:3 * r + 2] * o_s[sl]
                    + g[:, 3 * r + 2:3 * r + 3] * o_w[sl])
    o_ref[...] = jnp.concatenate(outs, axis=1).astype(o_ref.dtype)


def _rel_bucket(dist):
    n = jnp.maximum(dist, 0)
    max_exact = N_BUCKETS // 2
    large = max_exact + (jnp.log(jnp.maximum(n, 1).astype(F32) / max_exact)
                         / math.log(MAX_DISTANCE / max_exact) * (N_BUCKETS - max_exact)).astype(jnp.int32)
    large = jnp.minimum(large, N_BUCKETS - 1)
    return jnp.where(n < max_exact, n, large)


def _bias_tiles(tab, dist):
    onehot = (_rel_bucket(dist)[..., None] == jnp.arange(N_BUCKETS)).astype(F32)
    t = jnp.einsum('...b,bgr->...gr', onehot, tab, precision=lax.Precision.HIGHEST)
    nd = dist.ndim
    perm = (nd,) + tuple(range(nd - 2)) + (nd + 1, nd - 2, nd - 1)
    t = jnp.transpose(t, perm)
    return t.reshape(t.shape[:-3] + (t.shape[-3] * t.shape[-2], t.shape[-1]))


def _cover(ncp, nsp):
    c0 = jnp.arange(ncp)[:, None] * CMP_STRIDE
    s0 = jnp.arange(nsp)[None, :] * SEL_LEN
    ov = jnp.minimum(c0 + CMP_LEN, s0 + SEL_LEN) - jnp.maximum(c0, s0)
    return (jnp.maximum(ov, 0).astype(F32) / CMP_LEN).astype(MXU_DTYPE)


def _compress(kv_cmp, cmp_pe, cmp_w1, cmp_w2):
    B, L = kv_cmp.shape[:2]
    r = CMP_LEN // CMP_STRIDE
    nch = L // CMP_STRIDE
    nc = nch - r + 1
    chunks = kv_cmp[:, :nch * CMP_STRIDE].reshape(B, nch, CMP_STRIDE, 2, N_KV, HEAD_DIM)
    blocks = jnp.concatenate([chunks[:, m:m + nc] for m in range(r)], axis=2)
    blocks = blocks + cmp_pe[None, None, :, :, None, :]
    flat = jnp.transpose(blocks, (0, 1, 3, 4, 2, 5)).reshape(B, nc, 2, N_KV, CMP_LEN * HEAD_DIM)
    rnd = lambda v: v.astype(MXU_DTYPE).astype(F32)
    hid = jax.nn.silu(jnp.einsum('bnsgf,sfe->bnsge', rnd(flat), rnd(cmp_w1)))
    comp = jnp.einsum('bnsge,sed->bnsgd', rnd(hid), rnd(cmp_w2))
    return comp[:, :, 0], comp[:, :, 1]


def nsa_prompt(z, B, T, lay, tabs, cmp_pe, cmp_w1, cmp_w2):
    R, G = lay['R'], N_KV
    bq = Q_BLOCK
    nqb = T // bq
    N = B * T
    rows = R * bq
    kv = z[:, lay['okv']:lay['okv'] + 6 * G * HEAD_DIM].reshape(B, T, 6, G, HEAD_DIM)
    kcb, vcb = _compress(kv[:, :, :2], cmp_pe, cmp_w1, cmp_w2)
    nc = kcb.shape[1]
    ncp = tabs['ncp']
    pad = ((0, 0), (0, 0), (0, ncp - nc), (0, 0))
    kcb = jnp.pad(jnp.transpose(kcb, (0, 2, 1, 3)), pad).astype(MXU_DTYPE)
    vcb = jnp.pad(jnp.transpose(vcb, (0, 2, 1, 3)), pad).astype(MXU_DTYPE)
    win = kv[:, :, 4:].reshape(B, T, 2 * G * HEAD_DIM)
    win_pad = jnp.concatenate([jnp.zeros((B, WINDOW, 2 * G * HEAD_DIM), F32), win], axis=1)
    gz = z[:, lay['og']:lay['og'] + 3 * G * R].reshape(N, G, 3 * R)
    gz = jnp.pad(jnp.transpose(gz, (1, 0, 2)), ((0, 0), (0, 0), (0, LANE - 3 * R)))
    ns = -(-T // SEL_LEN)
    nsr = tabs['cover_t'].shape[0]
    qcol = lay['oq'] // (R * HEAD_DIM)
    kcol = lay['okv'] // HEAD_DIM
    wb = WINDOW + bq
    kernel = functools.partial(_nsa_kernel, R=R, bq=bq, ns=ns, n_top=min(SEL_TOP, ns))
    return pl.pallas_call(
        kernel,
        grid=(B, G, nqb),
        in_specs=[
            pl.BlockSpec((bq, R * HEAD_DIM), lambda b, g, i: (b * nqb + i, qcol + g)),
            pl.BlockSpec((None, bq, LANE), lambda b, g, i: (g, b * nqb + i, 0)),
            pl.BlockSpec((None, None, ncp, HEAD_DIM), lambda b, g, i: (b, g, 0, 0)),
            pl.BlockSpec((None, None, ncp, HEAD_DIM), lambda b, g, i: (b, g, 0, 0)),
            pl.BlockSpec((T, HEAD_DIM), lambda b, g, i: (b, kcol + 2 * G + g)),
            pl.BlockSpec((T, HEAD_DIM), lambda b, g, i: (b, kcol + 3 * G + g)),
            pl.BlockSpec((None, T + WINDOW, HEAD_DIM), lambda b, g, i: (b, 0, g)),
            pl.BlockSpec((None, T + WINDOW, HEAD_DIM), lambda b, g, i: (b, 0, G + g)),
            pl.BlockSpec((None, None, rows, ncp), lambda b, g, i: (g, i, 0, 0)),
            pl.BlockSpec((None, 3, rows, Q_BLOCK), lambda b, g, i: (g, 0, 0, 0)),
            pl.BlockSpec((None, rows, wb), lambda b, g, i: (g, 0, 0)),
            pl.BlockSpec((nsr, ncp), lambda b, g, i: (0, 0)),
        ],
        out_specs=pl.BlockSpec((bq, R * HEAD_DIM), lambda b, g, i: (b * nqb + i, g)),
        out_shape=jax.ShapeDtypeStruct((N, G * R * HEAD_DIM), MXU_DTYPE),
        scratch_shapes=[pltpu.VMEM((nqb, rows, Q_BLOCK), F32), pltpu.VMEM((rows, Q_BLOCK), F32),
                        pltpu.VMEM((rows, Q_BLOCK), F32), pltpu.VMEM((rows, HEAD_DIM), F32)],
        compiler_params=_cparams(("parallel", "parallel", "arbitrary")),
    )(z, gz, kcb, vcb, z, z, win_pad, win_pad, tabs['bias_c'], tabs['bias_s'], tabs['bias_w'], tabs['cover_t'])


def nsa_prompt_tables(rel_bias, T, R):
    G = N_KV
    bq = Q_BLOCK
    nqb = T // bq
    lut = rel_bias.astype(F32).reshape(N_BUCKETS, G, R)
    nc = T // CMP_STRIDE - CMP_LEN // CMP_STRIDE + 1
    ncp = -(-nc // LANE) * LANE
    ns = -(-T // SEL_LEN)
    nsr = -(-ns // SUBLANE) * SUBLANE
    i = jnp.arange(bq)[:, None]
    t = jnp.arange(T).reshape(nqb, bq, 1)
    cmp_end = (jnp.arange(ncp) * CMP_STRIDE + CMP_LEN - 1)[None, None, :]
    bias_c = _bias_tiles(lut, t - cmp_end)
    j = jnp.arange(Q_BLOCK)[None, :]
    d_s = jnp.stack([i - j, Q_BLOCK + i - j, jnp.full((bq, Q_BLOCK), MAX_DISTANCE)])
    bias_s = _bias_tiles(lut, d_s)
    jw = jnp.arange(WINDOW + bq)[None, :]
    bias_w = _bias_tiles(lut, i - jw + WINDOW)
    return dict(bias_c=bias_c, bias_s=bias_s, bias_w=bias_w, cover_t=_cover(ncp, nsr).T, ncp=ncp)


ROWS_PER_TOKEN = 4 * N_KV
CMP_PAGES_PER_STEP = 8


def _cmp_kernel(pt_ref, *refs, n_pg, chunks_per_page):
    page_refs = refs[:n_pg]
    pe_ref, w1_ref, o_ref, lhs_scr = refs[n_pg], refs[n_pg + 1], refs[n_pg + 2], refs[n_pg + 3]
    cpp = chunks_per_page
    stride = CMP_STRIDE * ROWS_PER_TOKEN
    for k in range(n_pg):
        for s in range(2):
            for g in range(N_KV):
                r0 = g * n_pg * cpp + k * cpp
                for j in range(CMP_STRIDE):
                    piece = page_refs[k][pl.ds(j * ROWS_PER_TOKEN + s * N_KV + g, cpp, stride=stride), :]
                    for h in range(2):
                        lhs_scr[h, s, r0:r0 + cpp, j * HEAD_DIM:(j + 1) * HEAD_DIM] = (
                            piece + pe_ref[h * CMP_STRIDE + j, s:s + 1, :])
    half = CMP_STRIDE * HEAD_DIM
    nrow = n_pg * cpp
    for s in range(2):
        top = jnp.dot(lhs_scr[0, s].astype(MXU_DTYPE), w1_ref[s, :half, :].astype(MXU_DTYPE),
                      preferred_element_type=F32)
        bot = jnp.dot(lhs_scr[1, s].astype(MXU_DTYPE), w1_ref[s, half:, :].astype(MXU_DTYPE),
                      preferred_element_type=F32)
        for g in range(N_KV):
            o_ref[s, 0, g] = top[g * nrow:(g + 1) * nrow]
            o_ref[s, 1, g] = bot[g * nrow:(g + 1) * nrow]


def decode_compress(cache2, page_table, cmp_pe, cmp_w1, layer):
    bs, n_pages = page_table.shape
    prow = cache2.shape[2]
    cpp = prow // ROWS_PER_TOKEN // CMP_STRIDE
    n_pg = math.gcd(CMP_PAGES_PER_STEP, n_pages)
    nch = n_pages * cpp
    page_specs = [pl.BlockSpec((None, None, prow, HEAD_DIM),
                               functools.partial(lambda b, i, pt, k: (layer, pt[b, i * n_pg + k], 0, 0), k=k))
                  for k in range(n_pg)]
    grid_spec = pltpu.PrefetchScalarGridSpec(
        num_scalar_prefetch=1,
        grid=(bs, n_pages // n_pg),
        in_specs=page_specs + [pl.BlockSpec((None, CMP_LEN, 2, HEAD_DIM), lambda b, i, pt: (layer, 0, 0, 0)),
                               pl.BlockSpec((None, 2, CMP_LEN * HEAD_DIM, HEAD_DIM), lambda b, i, pt: (layer, 0, 0, 0))],
        out_specs=pl.BlockSpec((None, 2, 2, N_KV, n_pg * cpp, HEAD_DIM), lambda b, i, pt: (b, 0, 0, 0, i, 0)),
        scratch_shapes=[pltpu.VMEM((2, 2, N_KV * n_pg * cpp, CMP_STRIDE * HEAD_DIM), F32)],
    )
    return pl.pallas_call(
        functools.partial(_cmp_kernel, n_pg=n_pg, chunks_per_page=cpp),
        grid_spec=grid_spec,
        out_shape=jax.ShapeDtypeStruct((bs, 2, 2, N_KV, nch, HEAD_DIM), F32),
        compiler_params=_cparams(("parallel", "arbitrary")),
    )(page_table, *([cache2] * n_pg), cmp_pe, cmp_w1)


def _decode_kernel(pt_ref, z_ref, rt_ref, parts_ref, page_ref, win_ref, w2_ref, bc_ref, bs_ref, bw_ref,
                   cov_ref, o_ref, q_scr, new_scr, sel_scr, m_scr, l_scr, acc_scr, oc_scr, st_scr,
                   *, lay, T, R, past_len, n_pages, ns, n_top, wb):
    p = pl.program_id(1)
    G = N_KV
    rows = R * T
    scale = HEAD_DIM ** -0.5
    page = page_ref.shape[0] // ROWS_PER_TOKEN
    t_row = rt_ref[...]
    q_pos_row = past_len + t_row
    nsp = cov_ref.shape[1]
    kvw = N_KV * HEAD_DIM

    @pl.when(p == 0)
    def _init():
        z = z_ref[...]
        new_scr[...] = jnp.zeros(new_scr.shape, F32)
        for g in range(G):
            for r in range(R):
                c0 = lay['oq'] + (g * R + r) * HEAD_DIM
                q_scr[g, r * T:(r + 1) * T, :] = z[:, c0:c0 + HEAD_DIM]
            for si, s in enumerate((2, 3, 4, 5)):
                c0 = lay['okv'] + s * kvw + g * HEAD_DIM
                new_scr[si, g, 0:T, :] = z[:, c0:c0 + HEAD_DIM]
        nch = parts_ref.shape[3]
        cmp_end = lax.broadcasted_iota(jnp.int32, (1, nch), 1) * CMP_STRIDE + (CMP_LEN - 1)
        blk = lax.broadcasted_iota(jnp.int32, (rows, nsp), 1)
        blk_f = blk.astype(F32)
        cur = jnp.right_shift(q_pos_row, int(math.log2(SEL_LEN)))
        forced = (blk == 0) | ((blk <= cur) & (blk > cur - N_LOCAL_SEL))
        valid = blk * SEL_LEN <= q_pos_row
        cov = cov_ref[...]
        for g in range(G):
            comp = []
            for s in range(2):
                pre = parts_ref[s, 0, g] + pltpu.roll(parts_ref[s, 1, g], nch - 1, 0)
                hid = pre * jax.nn.sigmoid(pre)
                comp.append(jnp.dot(hid.astype(MXU_DTYPE), w2_ref[s].astype(MXU_DTYPE),
                                    preferred_element_type=F32).astype(MXU_DTYPE))
            qr = q_scr[g].astype(MXU_DTYPE)
            s_c = _dot_nt(qr, comp[0]) * scale + bc_ref[g]
            p_c = _softmax_rows(s_c, cmp_end <= q_pos_row).astype(MXU_DTYPE)
            oc_scr[g] = jnp.dot(p_c, comp[1], preferred_element_type=F32)
            imp = jnp.dot(p_c, cov, preferred_element_type=F32)
            imp_all = imp
            for r in range(1, R):
                imp_all = imp_all + pltpu.roll(imp, r * T, 0)
            score = jnp.where(forced, FORCE_SCORE, jnp.where(valid, imp_all, -1.0))
            score = jnp.where(blk < ns, score, -2.0)
            sel = jnp.zeros((rows, nsp), F32)
            for _ in range(n_top):
                mx = jnp.max(score, axis=-1, keepdims=True)
                first = jnp.min(jnp.where(score == mx, blk_f, float(nsp)), axis=-1, keepdims=True)
                hit = blk_f == first
                sel = jnp.where(hit, 1.0, sel)
                score = jnp.where(hit, -3.0e38, score)
            sel_scr[g] = sel
        m_scr[...] = jnp.full(m_scr.shape, NEG_INF, F32)
        l_scr[...] = jnp.zeros(l_scr.shape, F32)
        acc_scr[...] = jnp.zeros(acc_scr.shape, F32)

    def flash_update(g, s, ok, v):
        s = jnp.where(ok, s, NEG_INF)
        m_old = m_scr[g]
        m_new = jnp.maximum(m_old, jnp.max(s, axis=-1, keepdims=True))
        alpha = jnp.exp(m_old - m_new)
        pe = jnp.where(ok, jnp.exp(s - m_new), 0.0)
        l_scr[g] = alpha * l_scr[g] + jnp.sum(pe, axis=-1, keepdims=True)
        acc_scr[g] = alpha * acc_scr[g] + jnp.dot(pe.astype(MXU_DTYPE), v, preferred_element_type=F32)
        m_scr[g] = m_new

    e_row = lax.broadcasted_iota(jnp.int32, (nsp, page), 0)
    e_col = jnp.right_shift(lax.broadcasted_iota(jnp.int32, (nsp, page), 1), int(math.log2(SEL_LEN)))
    expand = (e_row == p * (page // SEL_LEN) + e_col).astype(MXU_DTYPE)
    last = p == n_pages - 1
    for g in range(G):
        k = page_ref[pl.ds(2 * N_KV + g, page, stride=ROWS_PER_TOKEN), :].astype(MXU_DTYPE)
        v = page_ref[pl.ds(3 * N_KV + g, page, stride=ROWS_PER_TOKEN), :].astype(MXU_DTYPE)
        qr = q_scr[g].astype(MXU_DTYPE)
        s = _dot_nt(qr, k) * scale + jnp.where(last, bs_ref[g, 1], bs_ref[g, 2])
        picked = jnp.dot(sel_scr[g].astype(MXU_DTYPE), expand, preferred_element_type=F32)
        flash_update(g, s, picked > 0.5, v)

    @pl.when(last)
    def _finish():
        z = z_ref[...]
        j_new = lax.broadcasted_iota(jnp.int32, (1, new_scr.shape[2]), 1)
        ok_new = (j_new <= t_row) & (j_new < T)
        j_w = lax.broadcasted_iota(jnp.int32, (1, wb), 1)
        d_w = wb + t_row - j_w
        ok_w = (d_w >= 0) & (d_w < WINDOW) & (past_len - wb + j_w >= 0)
        gates = jax.nn.sigmoid(z[:, lay['og']:lay['og'] + LANE])
        for g in range(G):
            qr = q_scr[g].astype(MXU_DTYPE)
            s_n = _dot_nt(qr, new_scr[0, g].astype(MXU_DTYPE)) * scale + bs_ref[g, 0]
            flash_update(g, s_n, ok_new, new_scr[1, g].astype(MXU_DTYPE))
            o_s = acc_scr[g] / jnp.maximum(l_scr[g], 1e-30)
            kw = win_ref[pl.ds(g, wb, stride=2 * N_KV), :].astype(MXU_DTYPE)
            vw = win_ref[pl.ds(N_KV + g, wb, stride=2 * N_KV), :].astype(MXU_DTYPE)
            s1 = jnp.where(ok_w, _dot_nt(qr, kw) * scale + bw_ref[g], NEG_INF)
            s2 = jnp.where(ok_new, _dot_nt(qr, new_scr[2, g].astype(MXU_DTYPE)) * scale + bs_ref[g, 0], NEG_INF)
            mw = jnp.maximum(jnp.max(s1, axis=-1, keepdims=True), jnp.max(s2, axis=-1, keepdims=True))
            e1 = jnp.where(ok_w, jnp.exp(s1 - mw), 0.0)
            e2 = jnp.where(ok_new, jnp.exp(s2 - mw), 0.0)
            den = jnp.sum(e1, axis=-1, keepdims=True) + jnp.sum(e2, axis=-1, keepdims=True)
            o_w = (jnp.dot(e1.astype(MXU_DTYPE), vw, preferred_element_type=F32)
                   + jnp.dot(e2.astype(MXU_DTYPE), new_scr[3, g].astype(MXU_DTYPE), preferred_element_type=F32))
            o_w = o_w / jnp.maximum(den, 1e-30)
            st_scr[0] = oc_scr[g]
            st_scr[1] = o_s
            st_scr[2] = o_w
            for r in range(R):
                gc = (g * R + r) * 3
                rs = pl.ds(r * T, T)
                out = (gates[:, gc:gc + 1] * st_scr[0, rs, :] + gates[:, gc + 1:gc + 2] * st_scr[1, rs, :]
                       + gates[:, gc + 2:gc + 3] * st_scr[2, rs, :])
                c0 = (g * R + r) * HEAD_DIM
                o_ref[:, c0:c0 + HEAD_DIM] = out


def nsa_decode(zs3, cache2, page_table, win2, lay, tabs, cmp_pe, cmp_w1, cmp_w2, layer, past_len):
    bs, T, NZ = zs3.shape
    R, G = lay['R'], N_KV
    rows = R * T
    n_pages = page_table.shape[1]
    prow = cache2.shape[2]
    page = prow // ROWS_PER_TOKEN
    assert past_len % page == 0 and page % SEL_LEN == 0 and T < CMP_STRIDE and T <= SEL_LEN
    parts = decode_compress(cache2, page_table, cmp_pe, cmp_w1, layer)
    nch = parts.shape[4]
    wb = win2.shape[2] // (2 * N_KV)
    ns = -(-(past_len + T) // SEL_LEN)
    nsp = tabs['cover'].shape[1]
    npad = tabs['bias_s'].shape[-1]
    kernel = functools.partial(_decode_kernel, lay=lay, T=T, R=R, past_len=past_len, n_pages=n_pages, ns=ns,
                               n_top=min(SEL_TOP, ns), wb=wb)
    grid_spec = pltpu.PrefetchScalarGridSpec(
        num_scalar_prefetch=1,
        grid=(bs, n_pages),
        in_specs=[
            pl.BlockSpec((None, T, NZ), lambda b, p, pt: (b, 0, 0)),
            pl.BlockSpec((rows, 1), lambda b, p, pt: (0, 0)),
            pl.BlockSpec((None, 2, 2, G, nch, HEAD_DIM), lambda b, p, pt: (b, 0, 0, 0, 0, 0)),
            pl.BlockSpec((None, None, prow, HEAD_DIM), lambda b, p, pt: (layer, pt[b, p], 0, 0)),
            pl.BlockSpec((None, None, wb * 2 * N_KV, HEAD_DIM), lambda b, p, pt: (layer, b, 0, 0)),
            pl.BlockSpec((None, 2, HEAD_DIM, HEAD_DIM), lambda b, p, pt: (layer, 0, 0, 0)),
            pl.BlockSpec((G, rows, nch), lambda b, p, pt: (0, 0, 0)),
            pl.BlockSpec((G, 3, rows, npad), lambda b, p, pt: (0, 0, 0, 0)),
            pl.BlockSpec((G, rows, wb), lambda b, p, pt: (0, 0, 0)),
            pl.BlockSpec((nch, nsp), lambda b, p, pt: (0, 0)),
        ],
        out_specs=pl.BlockSpec((None, T, G * R * HEAD_DIM), lambda b, p, pt: (b, 0, 0)),
        scratch_shapes=[pltpu.VMEM((G, rows, HEAD_DIM), F32), pltpu.VMEM((4, G, npad, HEAD_DIM), F32),
                        pltpu.VMEM((G, rows, nsp), F32), pltpu.VMEM((G, rows, 1), F32),
                        pltpu.VMEM((G, rows, 1), F32), pltpu.VMEM((G, rows, HEAD_DIM), F32),
                        pltpu.VMEM((G, rows, HEAD_DIM), F32), pltpu.VMEM((3, rows, HEAD_DIM), F32)],
    )
    return pl.pallas_call(
        kernel,
        grid_spec=grid_spec,
        out_shape=jax.ShapeDtypeStruct((bs, T, G * R * HEAD_DIM), F32),
        compiler_params=_cparams(("parallel", "arbitrary")),
    )(page_table, zs3, tabs['row_t'], parts, cache2, win2, cmp_w2, tabs['bias_c'], tabs['bias_s'],
      tabs['bias_w'], tabs['cover'])


def nsa_decode_tables(rel_bias, T, R, past_len, wb, page):
    G = N_KV
    tab = rel_bias.astype(F32).reshape(N_BUCKETS, G, R)
    nch = past_len // CMP_STRIDE
    ns = -(-(past_len + T) // SEL_LEN)
    nsp = -(-ns // LANE) * LANE
    t = jnp.arange(T)[:, None]
    cmp_end = (jnp.arange(nch) * CMP_STRIDE + CMP_LEN - 1)[None, :]
    bias_c = _bias_tiles(tab, past_len + t - cmp_end)
    j = jnp.arange(page)[None, :]
    d_s = jnp.stack([t - j, page + t - j, jnp.full((T, page), MAX_DISTANCE)])
    bias_s = _bias_tiles(tab, d_s)
    jw = jnp.arange(wb)[None, :]
    bias_w = _bias_tiles(tab, wb + t - jw)
    row_t = jnp.tile(jnp.arange(T, dtype=jnp.int32), R).reshape(R * T, 1)
    return dict(bias_c=bias_c, bias_s=bias_s, bias_w=bias_w, cover=_cover(nch, nsp), row_t=row_t)


CONV_HIST = 32


def _conv_mix_kernel(ab_ref, ac_ref, ah_ref, bl_ref, bg_ref, zc_ref, cap_ref, cbp_ref, pp_ref, caw_ref, cbw_ref,
                     cbb_ref, lng_ref, lnb_ref, pw_ref, ps_ref, y_ref, cao_ref, cbo_ref, po_ref,
                     ea_scr, eb_scr, ec_scr, ear_scr, ebr_scr, *, tt, pos0):
    i = pl.program_id(1)
    H = CONV_HIST
    ka, kb, kp = caw_ref.shape[0] - 1, cbw_ref.shape[0] - 1, pp_ref.shape[0]
    da, db = ab_ref.shape[1], bl_ref.shape[1]
    rnd = lambda v: v.astype(MXU_DTYPE).astype(F32)

    @pl.when(i == 0)
    def _():
        for scr, prev, k in ((ea_scr, cap_ref, ka), (eb_scr, cbp_ref, kb), (ec_scr, pp_ref, kp)):
            scr[0:H, :] = jnp.zeros((H, scr.shape[1]), F32)
            scr[H - k:H, :] = prev[...]
        for scr, prev, k in ((ear_scr, cap_ref, ka), (ebr_scr, cbp_ref, kb)):
            scr[0:H, :] = jnp.zeros((H, scr.shape[1]), F32)
            scr[H - k:H, :] = rnd(prev[...])

    ua = ac_ref[...] * ah_ref[...]
    ea_scr[H:H + tt, :] = ua
    ear_scr[H:H + tt, :] = rnd(ua)
    conv = None
    for k in range(ka + 1):
        term = rnd(caw_ref[k:k + 1, :]) * ear_scr[pl.ds(H - ka + k, tt), :]
        conv = term if conv is None else conv + term
    y_ref[:, 0:da] = (ab_ref[...] * conv).astype(y_ref.dtype)

    ub = bl_ref[...] * jax.nn.sigmoid(bg_ref[...])
    eb_scr[H:H + tt, :] = ub
    ebr_scr[H:H + tt, :] = rnd(ub)
    cb = None
    for k in range(kb + 1):
        term = rnd(cbw_ref[k:k + 1, :]) * ebr_scr[pl.ds(H - kb + k, tt), :]
        cb = term if cb is None else cb + term
    cb = cb + cbb_ref[...]
    mu = jnp.mean(cb, axis=-1, keepdims=True)
    var = jnp.mean(jnp.square(cb - mu), axis=-1, keepdims=True)
    ln = (cb - mu) * lax.rsqrt(var + LN_EPS) * lng_ref[...] + lnb_ref[...]
    y_ref[:, da:da + db] = (ln * jax.nn.sigmoid(ln)).astype(y_ref.dtype)

    zc = zc_ref[...]
    ec_scr[H:H + tt, :] = zc
    pos = pos0 + i * tt + lax.broadcasted_iota(jnp.int32, (tt, 1), 0)
    cg = zc.shape[1] // len(POOL_WINDOWS)
    for gi, w in enumerate(POOL_WINDOWS):
        c0 = gi * cg
        ssum = None
        for k in range(w):
            term = ec_scr[pl.ds(H - k, tt), c0:c0 + cg]
            ssum = term if ssum is None else ssum + term
        cnt = jnp.minimum(w, pos + 1).astype(F32)
        pooled = ssum / cnt - zc[:, c0:c0 + cg]
        mixed = jnp.dot(pooled.astype(MXU_DTYPE), pw_ref[gi].astype(MXU_DTYPE), preferred_element_type=F32)
        y_ref[:, da + db + c0:da + db + c0 + cg] = (mixed * ps_ref[:, c0:c0 + cg]).astype(y_ref.dtype)

    for scr in (ea_scr, eb_scr, ec_scr, ear_scr, ebr_scr):
        tail = scr[pl.ds(tt, H), :]
        scr[0:H, :] = tail

    @pl.when(i == pl.num_programs(1) - 1)
    def _():
        cao_ref[...] = ea_scr[H - ka:H, :]
        cbo_ref[...] = eb_scr[H - kb:H, :]
        po_ref[...] = ec_scr[H - kp:H, :]


def conv_mix(z3, ca_prev, cb_prev, pool_prev, prm, lay, layer, pos0, out_dtype):
    B, T, _ = z3.shape
    DA, DB, DC = lay['DA'], lay['DB'], lay['DC']
    assert DA == DB == DC and lay['ob'] == 3 * DA and lay['oc'] == 5 * DA
    tt = _pick(T, (256, 128, 64, 32)) if T >= CONV_HIST else T
    nt = T // tt
    ka, kb, kp = ca_prev.shape[1], cb_prev.shape[1], pool_prev.shape[1]
    ngrp = len(POOL_WINDOWS)
    zcol = lambda c: pl.BlockSpec((None, tt, DA), lambda b, i: (b, i, c))
    seq = lambda k, w: pl.BlockSpec((None, k, w), lambda b, i: (b, 0, 0))
    lyr = lambda k, w: pl.BlockSpec((None, k, w), lambda b, i: (layer, 0, 0))
    vec = lambda a: a.reshape(a.shape[0], 1, a.shape[1])
    return pl.pallas_call(
        functools.partial(_conv_mix_kernel, tt=tt, pos0=pos0),
        grid=(B, nt),
        in_specs=[zcol(0), zcol(1), zcol(2), zcol(3), zcol(4), zcol(5),
                  seq(ka, DA), seq(kb, DB), seq(kp, DC),
                  lyr(ka + 1, DA), lyr(kb + 1, DB), lyr(1, DB), lyr(1, DB), lyr(1, DB),
                  pl.BlockSpec((None, ngrp, DC // ngrp, DC // ngrp), lambda b, i: (layer, 0, 0, 0)), lyr(1, DC)],
        out_specs=[pl.BlockSpec((None, tt, DA + DB + DC), lambda b, i: (b, i, 0)),
                   seq(ka, DA), seq(kb, DB), seq(kp, DC)],
        out_shape=[jax.ShapeDtypeStruct((B, T, DA + DB + DC), out_dtype),
                   jax.ShapeDtypeStruct((B, ka, DA), F32), jax.ShapeDtypeStruct((B, kb, DB), F32),
                   jax.ShapeDtypeStruct((B, kp, DC), F32)],
        scratch_shapes=[pltpu.VMEM((CONV_HIST + tt, DA), F32), pltpu.VMEM((CONV_HIST + tt, DB), F32),
                        pltpu.VMEM((CONV_HIST + tt, DC), F32), pltpu.VMEM((CONV_HIST + tt, DA), F32),
                        pltpu.VMEM((CONV_HIST + tt, DB), F32)],
        compiler_params=_cparams(("parallel", "arbitrary")),
    )(z3, z3, z3, z3, z3, z3, ca_prev, cb_prev, pool_prev, prm['conv_a_w'], prm['conv_b_w'],
      vec(prm['conv_b_b']), vec(prm['ln_b_g']), vec(prm['ln_b_b']), prm['pool_w'], vec(prm['pool_scale']))


def _layout(prm, D):
    DA = prm['conv_a_w'].shape[-1]
    DB = prm['conv_b_w'].shape[-1]
    DC = prm['pool_scale'].shape[-1]
    NH = prm['rel_bias'].shape[1]
    R = NH // N_KV
    DD = NH * HEAD_DIM
    ob = 3 * DA
    oc = ob + 2 * DB
    oq = oc + DC
    okv = oq + DD
    og = okv + 6 * N_KV * HEAD_DIM
    assert oq % (R * HEAD_DIM) == 0 and okv % HEAD_DIM == 0
    return dict(DA=DA, DB=DB, DC=DC, R=R, DD=DD, ob=ob, oc=oc, oq=oq, okv=okv, og=og, n_in=og + 3 * NH)


def _mixer(z, B, T, l, pos0, prev, win_keep, prm, lay, tabs):
    kv_past, win_prev, ca_prev, cb_prev, pool_prev = prev
    z3 = z.reshape(B, T, z.shape[-1])
    small = T < 2 * SUBLANE
    y_abc, ca, cb, pool = conv_mix(z3, ca_prev, cb_prev, pool_prev, prm, lay, l, pos0, F32 if small else MXU_DTYPE)
    kv = z3[..., lay['okv']:lay['okv'] + 6 * N_KV * HEAD_DIM].reshape(B, T, 6, N_KV, HEAD_DIM)
    win_rows = jnp.concatenate([win_prev, kv[:, :, 4:]], axis=1)
    if kv_past is None:
        y_d = nsa_prompt(z, B, T, lay, tabs, prm['cmp_pe'][l], prm['cmp_w1'][l], prm['cmp_w2'][l])
    else:
        y_d = nsa_decode(z3, kv_past['cache2'], kv_past['page_table'], kv_past['win2'], lay, tabs,
                         prm['cmp_pe'], prm['cmp_w1'], prm['cmp_w2'], l, pos0)
    y_abc = y_abc.reshape(B * T, -1).astype(MXU_DTYPE)
    y_d = y_d.reshape(B * T, -1).astype(MXU_DTYPE)
    state = (kv[:, :, :4], win_rows[:, -win_keep:], ca, cb, pool)
    return (y_abc, y_d), state


def _run_trunk(x3, mods, pos0, prev_fn, win_keep, prm, lay, tabs):
    B, T, D = x3.shape
    N = B * T
    depth = prm['norm_w'].shape[0]
    x = x3.reshape(N, D)
    tr = _pick(N, (256, 128, 64, 32, 16, 8))
    if T % tr == 0:
        per_seq = lambda m: m[:, None, :]
    else:
        assert tr == N
        per_seq = lambda m: jnp.repeat(m, T, axis=0)[None]
    splits = []
    acc = 0
    for wdt in (lay['DA'], lay['DB'], lay['DC'], lay['DD']):
        splits.append((acc, acc + wdt))
        acc += wdt
    outs = ([], [], [], [], [])
    nw = prm['norm_w']
    mod = mods[0]
    h = rms_mod(x, nw[0, 0][None], per_seq(1.0 + mod[:, 1]), per_seq(mod[:, 0]), T, tr)
    for l in range(depth):
        mod = mods[l]
        z = matmul(h, prm['w_in'], l)
        (y_abc, y_d), st = _mixer(z, B, T, l, pos0, prev_fn(l), win_keep, prm, lay, tabs)
        merged = merge(h, y_abc, y_d, prm['w_mgate'], prm['w_branch'], l, tuple(splits))
        y = matmul(merged, prm['w_out'], l)
        x, h2 = resid_norm(x, y, nw[l, 1][None], per_seq(mod[:, 2]), nw[l, 2][None],
                           per_seq(1.0 + mod[:, 4]), per_seq(mod[:, 3]), T, tr)
        f = moe_ffn(h2, prm, l)
        ln = min(l + 1, depth - 1)
        modn = mods[ln]
        x, h = resid_norm(x, f, nw[l, 3][None], per_seq(mod[:, 5]), nw[ln, 0][None],
                          per_seq(1.0 + modn[:, 1]), per_seq(modn[:, 0]), T, tr)
        for a, s in zip(outs, st):
            a.append(s)
    return x.reshape(B, T, D), [jnp.stack(a) for a in outs]


def kernel(x_prompt, x_sample, cache_kv, state_kv_win, state_conv_a, state_conv_b, state_pool, page_table,
           c_prompt, c_sample, rel_bias, ada_w, ada_b, norm_w, w_in, conv_a_w, conv_b_w, conv_b_b, ln_b_g, ln_b_b,
           pool_w, pool_scale, cmp_pe, cmp_w1, cmp_w2, w_branch, w_mgate, w_out, w_router, b_router,
           w_exp_gate, w_exp_up, w_exp_down, w_sh_gate, w_sh_up, w_sh_down):
    bp, tp, D = x_prompt.shape
    bs, ts, _ = x_sample.shape
    depth = norm_w.shape[0]
    dt = x_prompt.dtype
    past_len = page_table.shape[1] * cache_kv.shape[2]
    E = b_router.shape[-1]

    prm = dict(rel_bias=rel_bias, norm_w=norm_w, conv_a_w=conv_a_w, conv_b_w=conv_b_w, conv_b_b=conv_b_b,
               ln_b_g=ln_b_g, ln_b_b=ln_b_b, pool_w=pool_w, pool_scale=pool_scale, cmp_pe=cmp_pe, cmp_w1=cmp_w1,
               cmp_w2=cmp_w2, b_router=b_router)
    lay = _layout(prm, D)
    n_in = lay['n_in']
    nz = lay['og'] + 2 * LANE
    prm['w_in'] = jnp.pad(w_in, ((0, 0), (0, 0), (0, nz - n_in))).astype(MXU_DTYPE)
    prm['w_mgate'] = w_mgate.astype(MXU_DTYPE)
    prm['w_branch'] = w_branch.astype(MXU_DTYPE)
    prm['w_out'] = w_out.astype(MXU_DTYPE)
    prm['w_router'] = jnp.pad(w_router, ((0, 0), (0, 0), (0, LANE - E))).astype(MXU_DTYPE)
    prm['w_exp_gate'] = w_exp_gate.astype(MXU_DTYPE)
    prm['w_exp_up'] = w_exp_up.astype(MXU_DTYPE)
    prm['w_exp_down'] = w_exp_down.astype(MXU_DTYPE)
    prm['w_sh_gu'] = jnp.concatenate([w_sh_gate, w_sh_up], axis=-1).astype(MXU_DTYPE)
    prm['w_sh_down'] = w_sh_down.astype(MXU_DTYPE)

    nb = bp + bs
    nbp = -(-nb // 16) * 16
    cs = jnp.pad(jax.nn.silu(jnp.concatenate([c_prompt, c_sample], axis=0)), ((0, nbp - nb), (0, 0)))
    cs = cs.astype(MXU_DTYPE)
    mods_p, mods_s = [], []
    for l in range(depth):
        m = (matmul(cs, ada_w, l, tn=_pick(6 * D, (512, 256, 128)))[:nb] + ada_b[l]).reshape(nb, 6, D)
        mods_p.append(m[:bp])
        mods_s.append(m[bp:])

    tabs = nsa_prompt_tables(rel_bias, tp, lay['R'])

    def prompt_prev(l):
        return (None,
                jnp.zeros((bp, WINDOW, 2, N_KV, HEAD_DIM), dt),
                jnp.zeros((bp, conv_a_w.shape[1] - 1, lay['DA']), dt),
                jnp.zeros((bp, conv_b_w.shape[1] - 1, lay['DB']), dt),
                jnp.zeros((bp, state_pool.shape[2], lay['DC']), dt))

    page = cache_kv.shape[2]
    wbuf = state_kv_win.shape[2]
    paged = dict(cache2=cache_kv.reshape(depth, cache_kv.shape[1], page * ROWS_PER_TOKEN, HEAD_DIM),
                 win2=state_kv_win.reshape(depth, bs, wbuf * 2 * N_KV, HEAD_DIM), page_table=page_table)
    tabs_s = nsa_decode_tables(rel_bias, ts, lay['R'], past_len, wbuf, page)

    def sample_prev(l):
        return (paged, state_kv_win[l], state_conv_a[l], state_conv_b[l], state_pool[l])

    y_prompt, st_p = _run_trunk(x_prompt, mods_p, 0, prompt_prev, min(WINDOW, tp), prm, lay, tabs)
    y_sample, st_s = _run_trunk(x_sample, mods_s, past_len, sample_prev, wbuf, prm, lay, tabs_s)
    kv_p, win_p, ca_p, cb_p, pl_p = st_p
    kv_s, win_s, ca_s, cb_s, pl_s = st_s
    return (y_prompt, y_sample, kv_p, kv_s, win_p, win_s, ca_p, ca_s, cb_p, cb_s, pl_p, pl_s)
```

```python
import functools
import math

import jax
import jax.numpy as jnp
from jax import lax
from jax.experimental import pallas as pl
from jax.experimental.pallas import tpu as pltpu

F32 = jnp.float32
MXU_DTYPE = jnp.bfloat16
VMEM_LIMIT_BYTES = 56 * 1024 * 1024
LANE = 128
SUBLANE = 8

RMS_EPS = 1e-6
LN_EPS = 1e-5
N_KV = 4
HEAD_DIM = 128
POOL_WINDOWS = (2, 4, 8, 16)
CMP_LEN = 32
CMP_STRIDE = 16
SEL_LEN = 64
SEL_TOP = 16
N_LOCAL_SEL = 2
WINDOW = 512
Q_BLOCK = 128
SEL_CHUNK = 256
FORCE_SCORE = 1e4
NEG_INF = -1e30
N_BUCKETS = 32
MAX_DISTANCE = 128
TOP_K = 8
MOE_TILE_ROWS = 256


def _pick(n, cands):
    for c in cands:
        if n % c == 0:
            return c
    return n


def _cparams(sem):
    return pltpu.CompilerParams(dimension_semantics=sem, vmem_limit_bytes=VMEM_LIMIT_BYTES)


def _mm_kernel(a_ref, w_ref, o_ref):
    w = w_ref[...].astype(MXU_DTYPE)
    o_ref[...] = jnp.dot(a_ref[...], w, preferred_element_type=F32).astype(o_ref.dtype)


def matmul(a, w, layer=None, out_dtype=F32, tm=None, tn=None):
    M, K = a.shape
    N = w.shape[-1]
    tm = tm or _pick(M, (1024, 512, 256, 128, 64, 32, 16, 8))
    tn = tn or _pick(N, (512, 768, 384, 256, 128))
    if w.ndim == 3:
        w_spec = pl.BlockSpec((None, K, tn), lambda i, j: (layer, 0, j))
    else:
        w_spec = pl.BlockSpec((K, tn), lambda i, j: (0, j))
    return pl.pallas_call(
        _mm_kernel,
        grid=(M // tm, N // tn),
        in_specs=[pl.BlockSpec((tm, K), lambda i, j: (i, 0)), w_spec],
        out_specs=pl.BlockSpec((tm, tn), lambda i, j: (i, j)),
        out_shape=jax.ShapeDtypeStruct((M, N), out_dtype),
        compiler_params=_cparams(("parallel", "parallel")),
    )(a, w)


def _rms(x, w):
    return x * lax.rsqrt(jnp.mean(x * x, axis=-1, keepdims=True) + RMS_EPS) * w


def _rms_mod_kernel(x_ref, nw_ref, sc_ref, sh_ref, h_ref):
    h = _rms(x_ref[...], nw_ref[...]) * sc_ref[0] + sh_ref[0]
    h_ref[...] = h.astype(h_ref.dtype)


def _mod_spec(mod, rows_per_seq, tr):
    Bm, Rm, D = mod.shape
    if Rm == 1:
        per = rows_per_seq // tr
        return pl.BlockSpec((1, 1, D), lambda i: (i // per, 0, 0))
    return pl.BlockSpec((1, Rm, D), lambda i: (i, 0, 0))


def rms_mod(x, nw, sc, sh, rows_per_seq, tr):
    M, D = x.shape
    return pl.pallas_call(
        _rms_mod_kernel,
        grid=(M // tr,),
        in_specs=[pl.BlockSpec((tr, D), lambda i: (i, 0)), pl.BlockSpec((1, D), lambda i: (0, 0)),
                  _mod_spec(sc, rows_per_seq, tr), _mod_spec(sh, rows_per_seq, tr)],
        out_specs=pl.BlockSpec((tr, D), lambda i: (i, 0)),
        out_shape=jax.ShapeDtypeStruct((M, D), MXU_DTYPE),
        compiler_params=_cparams(("parallel",)),
    )(x, nw, sc, sh)


def _resid_kernel(x_ref, y_ref, nwy_ref, gate_ref, nwn_ref, sc_ref, sh_ref, xo_ref, h_ref):
    x = x_ref[...] + gate_ref[0] * _rms(y_ref[...], nwy_ref[...])
    xo_ref[...] = x
    h_ref[...] = (_rms(x, nwn_ref[...]) * sc_ref[0] + sh_ref[0]).astype(h_ref.dtype)


def resid_norm(x, y, nwy, gate, nwn, sc, sh, rows_per_seq, tr):
    M, D = x.shape
    row = pl.BlockSpec((tr, D), lambda i: (i, 0))
    vec = pl.BlockSpec((1, D), lambda i: (0, 0))
    return pl.pallas_call(
        _resid_kernel,
        grid=(M // tr,),
        in_specs=[row, row, vec, _mod_spec(gate, rows_per_seq, tr), vec,
                  _mod_spec(sc, rows_per_seq, tr), _mod_spec(sh, rows_per_seq, tr)],
        out_specs=[row, row],
        out_shape=[jax.ShapeDtypeStruct((M, D), F32), jax.ShapeDtypeStruct((M, D), MXU_DTYPE)],
        compiler_params=_cparams(("parallel",)),
    )(x, y, nwy, gate, nwn, sc, sh)


def _merge_kernel(h_ref, y1_ref, y2_ref, wg0, wg1, wg2, wg3, wb_ref, o_ref, *, splits):
    h = h_ref[...]
    c1 = y1_ref.shape[1]
    acc = None
    for (lo, hi), wg in zip(splits, (wg0, wg1, wg2, wg3)):
        gate = jax.nn.sigmoid(jnp.dot(h, wg[...], preferred_element_type=F32))
        y = y1_ref[:, lo:hi] if hi <= c1 else y2_ref[:, lo - c1:hi - c1]
        br = jnp.dot(y, wb_ref[lo:hi, :], preferred_element_type=F32)
        acc = gate * br if acc is None else acc + gate * br
    o_ref[...] = acc.astype(o_ref.dtype)


def merge(h, y1, y2, w_mgate, w_branch, layer, splits):
    M, D = h.shape
    C1, C2 = y1.shape[1], y2.shape[1]
    tm = _pick(M, (512, 256, 128, 64, 32, 16, 8))
    tn = _pick(D, (256, 128))
    nj = D // tn
    wg_specs = [pl.BlockSpec((None, D, tn), functools.partial(lambda i, j, k: (layer, 0, k * nj + j), k=k))
                for k in range(4)]
    return pl.pallas_call(
        functools.partial(_merge_kernel, splits=splits),
        grid=(M // tm, nj),
        in_specs=[pl.BlockSpec((tm, D), lambda i, j: (i, 0)), pl.BlockSpec((tm, C1), lambda i, j: (i, 0)),
                  pl.BlockSpec((tm, C2), lambda i, j: (i, 0))]
        + wg_specs + [pl.BlockSpec((None, C1 + C2, tn), lambda i, j: (layer, 0, j))],
        out_specs=pl.BlockSpec((tm, tn), lambda i, j: (i, j)),
        out_shape=jax.ShapeDtypeStruct((M, D), MXU_DTYPE),
        compiler_params=_cparams(("parallel", "parallel")),
    )(h, y1, y2, w_mgate, w_mgate, w_mgate, w_mgate, w_branch)


def _moe_kernel(pt_ref, pe_ref, lo_ref, hi_ref, first_ref, np_ref, x_ref, g_ref, wg_ref, wu_ref, wd_ref, o_ref):
    p = pl.program_id(0)

    @pl.when(p < np_ref[0])
    def _():
        x = x_ref[...]
        gp = jnp.dot(x, wg_ref[...], preferred_element_type=F32)
        up = jnp.dot(x, wu_ref[...], preferred_element_type=F32)
        hid = (gp * jax.nn.sigmoid(gp)) * up * g_ref[...]
        y = jnp.dot(hid.astype(MXU_DTYPE), wd_ref[...], preferred_element_type=F32)
        row = lax.broadcasted_iota(jnp.int32, (x.shape[0], 1), 0)
        mine = (row >= lo_ref[p]) & (row < hi_ref[p])

        @pl.when(first_ref[p] == 1)
        def _():
            o_ref[...] = jnp.where(mine, y, 0.0)

        @pl.when(first_ref[p] == 0)
        def _():
            o_ref[...] = jnp.where(mine, y, o_ref[...])


def moe_grouped(x_sorted, gate_sorted, pairs, w_eg, w_eu, w_ed, layer, tm):
    A, D = x_sorted.shape
    Fe = w_eg.shape[-1]
    pair_tile, pair_expert, lo, hi, first, n_pairs = pairs
    row_spec = lambda w: pl.BlockSpec((tm, w), lambda p, pt, pe, lo, hi, fi, npr: (pt[p], 0))
    w_spec = lambda a, b: pl.BlockSpec((None, None, a, b), lambda p, pt, pe, lo, hi, fi, npr: (layer, pe[p], 0, 0))
    grid_spec = pltpu.PrefetchScalarGridSpec(
        num_scalar_prefetch=6,
        grid=(pair_tile.shape[0],),
        in_specs=[row_spec(D), row_spec(1), w_spec(D, Fe), w_spec(D, Fe), w_spec(Fe, D)],
        out_specs=row_spec(D),
    )
    return pl.pallas_call(
        _moe_kernel,
        grid_spec=grid_spec,
        out_shape=jax.ShapeDtypeStruct((A, D), F32),
        compiler_params=_cparams(("arbitrary",)),
    )(pair_tile, pair_expert, lo, hi, first, n_pairs, x_sorted, gate_sorted, w_eg, w_eu, w_ed)


def moe_route_shared(h2, prm, layer):
    E = prm['b_router'].shape[-1]
    logits = matmul(h2, prm['w_router'], layer)[:, :E]
    aff = jax.nn.sigmoid(logits)
    _, idx = lax.top_k(aff + prm['b_router'][layer].astype(F32), TOP_K)
    top_aff = jnp.take_along_axis(aff, idx, axis=-1)
    wts = top_aff / jnp.sum(top_aff, axis=-1, keepdims=True)
    gu = matmul(h2, prm['w_sh_gu'], layer)
    Fs = gu.shape[1] // 2
    hs = (jax.nn.silu(gu[:, :Fs]) * gu[:, Fs:]).astype(MXU_DTYPE)
    shared = matmul(hs, prm['w_sh_down'], layer)
    return idx, wts, shared


def moe_routed(h2s, idxs, wtss, prm, layer):
    sizes = [h.shape[0] for h in h2s]
    h2 = jnp.concatenate(h2s, axis=0)
    idx = jnp.concatenate(idxs, axis=0)
    wts = jnp.concatenate(wtss, axis=0)
    N = h2.shape[0]
    E = prm['b_router'].shape[-1]
    tm = MOE_TILE_ROWS
    A = N * TOP_K
    A_pad = -(-A // tm) * tm
    i32 = jnp.int32
    e_flat = jnp.pad(idx.reshape(A).astype(i32), (0, A_pad - A), constant_values=E - 1)
    w_flat = jnp.pad(wts.reshape(A), (0, A_pad - A))
    a_ids = jnp.arange(A_pad, dtype=i32)
    e_sorted, a_sorted, w_sorted = lax.sort((e_flat, a_ids, w_flat), num_keys=1, is_stable=True)
    tok_sorted = jnp.where(a_sorted < A, a_sorted // TOP_K, 0)
    eids = jnp.arange(E, dtype=i32)
    start = jnp.sum(e_sorted[None, :] < eids[:, None], axis=1).astype(i32)
    counts = jnp.sum(e_sorted[None, :] == eids[:, None], axis=1).astype(i32)
    end = start + counts
    f_tile = start // tm
    n_pairs_e = jnp.where(counts > 0, (end - 1) // tm - f_tile + 1, 0)
    pair_end = jnp.cumsum(n_pairs_e)
    pair_off = pair_end - n_pairs_e
    n_pairs = pair_end[-1]
    p_max = A_pad // tm + E
    p = jnp.arange(p_max, dtype=i32)
    p_eff = jnp.minimum(p, n_pairs - 1)
    e_p = jnp.sum(p_eff[:, None] >= pair_end[None, :], axis=1).astype(i32)
    tile = f_tile[e_p] + p_eff - pair_off[e_p]
    valid = p < n_pairs
    lo = jnp.where(valid, jnp.maximum(start[e_p], tile * tm) - tile * tm, 0).astype(i32)
    hi = jnp.where(valid, jnp.minimum(end[e_p], (tile + 1) * tm) - tile * tm, 0).astype(i32)
    first = (jnp.concatenate([jnp.ones((1,), bool), tile[1:] != tile[:-1]]) & valid).astype(i32)
    pairs = (tile.astype(i32), e_p, lo, hi, first, n_pairs.reshape(1).astype(i32))

    x_sorted = h2[tok_sorted]
    y_sorted = moe_grouped(x_sorted, w_sorted[:, None], pairs, prm['w_exp_gate'], prm['w_exp_up'],
                           prm['w_exp_down'], layer, tm)
    _, inv = lax.sort((a_sorted, a_ids), num_keys=1)
    routed = jnp.sum(y_sorted[inv[:A].reshape(N, TOP_K)], axis=1)
    outs, o = [], 0
    for n in sizes:
        outs.append(routed[o:o + n])
        o += n
    return outs


def _softmax_rows(s, mask):
    s = jnp.where(mask, s, NEG_INF)
    m = jnp.max(s, axis=-1, keepdims=True)
    p = jnp.where(mask, jnp.exp(s - m), 0.0)
    return p / jnp.maximum(jnp.sum(p, axis=-1, keepdims=True), 1e-30)


def _dot_nt(a, b):
    return lax.dot_general(a, b, (((1,), (1,)), ((), ())), preferred_element_type=F32)


def _nsa_kernel(q_ref, gz_ref, kc_ref, vc_ref, ks_ref, vs_ref, kw_ref, vw_ref, bc_ref, bs_ref, bw_ref, covt_ref,
                o_ref, s_scr, m_scr, l_scr, acc_scr, *, R, bq, ns, n_top):
    qb = pl.program_id(2)
    s0 = pl.multiple_of(qb * bq, bq)
    rows = R * bq
    scale = HEAD_DIM ** -0.5
    q = q_ref[...]
    q_rows = jnp.concatenate([q[:, r * HEAD_DIM:(r + 1) * HEAD_DIM] for r in range(R)], axis=0).astype(MXU_DTYPE)
    i_tok = lax.broadcasted_iota(jnp.int32, (bq, 1), 0)
    i_row = jnp.concatenate([i_tok] * R, axis=0)
    q_pos_row = s0 + i_row

    ncp = kc_ref.shape[0]
    s_c = _dot_nt(q_rows, kc_ref[...]) * scale + bc_ref[...]
    cmp_end = lax.broadcasted_iota(jnp.int32, (1, ncp), 1) * CMP_STRIDE + (CMP_LEN - 1)
    p_c = _softmax_rows(s_c, cmp_end <= q_pos_row).astype(MXU_DTYPE)
    o_c = jnp.dot(p_c, vc_ref[...], preferred_element_type=F32)

    wb = WINDOW + bq
    kw = kw_ref[pl.ds(s0, wb), :].astype(MXU_DTYPE)
    vw = vw_ref[pl.ds(s0, wb), :].astype(MXU_DTYPE)
    s_w = _dot_nt(q_rows, kw) * scale + bw_ref[...]
    j_w = lax.broadcasted_iota(jnp.int32, (1, wb), 1)
    d_w = i_row - j_w + WINDOW
    m_w = (d_w >= 0) & (d_w < WINDOW) & (s0 - WINDOW + j_w >= 0)
    p_w = _softmax_rows(s_w, m_w)
    o_w = jnp.dot(p_w.astype(MXU_DTYPE), vw, preferred_element_type=F32)

    covt = covt_ref[...]
    imp_rows = _dot_nt(covt, p_c)
    imp = imp_rows[:, 0:bq]
    for r in range(1, R):
        imp = imp + imp_rows[:, r * bq:(r + 1) * bq]
    nsr = covt.shape[0]
    blk = lax.broadcasted_iota(jnp.int32, (nsr, bq), 0)
    blk_f = blk.astype(F32)
    q_pos = s0 + lax.broadcasted_iota(jnp.int32, (1, bq), 1)
    cur = jnp.right_shift(q_pos, int(math.log2(SEL_LEN)))
    forced = (blk == 0) | ((blk <= cur) & (blk > cur - N_LOCAL_SEL))
    valid = blk * SEL_LEN <= q_pos
    score = jnp.where(forced, FORCE_SCORE, jnp.where(valid, imp, -1.0))
    score = jnp.where(blk < ns, score, -2.0)
    sel = jnp.zeros((nsr, bq), F32)
    for _ in range(n_top):
        mx = jnp.max(score, axis=0, keepdims=True)
        first = jnp.min(jnp.where(score == mx, blk_f, float(nsr)), axis=0, keepdims=True)
        hit = blk_f == first
        sel = jnp.where(hit, 1.0, sel)
        score = jnp.where(hit, -3.0e38, score)

    ch = s_scr.shape[2]
    halves = ch // Q_BLOCK
    n_chunks = (s0 + bq + ch - 1) // ch
    j_lane = lax.broadcasted_iota(jnp.int32, (1, ch), 1)
    e_row = lax.broadcasted_iota(jnp.int32, (nsr, ch), 0)
    e_col = jnp.right_shift(lax.broadcasted_iota(jnp.int32, (nsr, ch), 1), int(math.log2(SEL_LEN)))
    blocks_per_chunk = ch // SEL_LEN
    m_scr[...] = jnp.full(m_scr.shape, NEG_INF, F32)

    def pass1(c, carry):
        k0 = pl.multiple_of(c * ch, ch)
        k = ks_ref[pl.ds(k0, ch), :].astype(MXU_DTYPE)
        tiles = []
        for hf in range(halves):
            d = qb - (c * halves + hf)
            tiles.append(jnp.where(d == 0, bs_ref[0], jnp.where(d == 1, bs_ref[1], bs_ref[2])))
        s = _dot_nt(q_rows, k) * scale + jnp.concatenate(tiles, axis=1)
        expand = (e_row == c * blocks_per_chunk + e_col).astype(F32)
        picked = lax.dot_general(sel, expand, (((0,), (0,)), ((), ())), preferred_element_type=F32)
        picked = jnp.concatenate([picked] * R, axis=0)
        ok = (picked > 0.5) & (k0 + j_lane <= q_pos_row)
        s = jnp.where(ok, s, NEG_INF)
        s_scr[c] = s
        m_scr[...] = jnp.maximum(m_scr[...], s)
        return carry

    lax.fori_loop(0, n_chunks, pass1, 0)
    m_s = jnp.max(m_scr[...], axis=-1, keepdims=True)
    l_scr[...] = jnp.zeros(l_scr.shape, F32)
    acc_scr[...] = jnp.zeros(acc_scr.shape, F32)

    def pass2(c, carry):
        p = jnp.exp(s_scr[c] - m_s)
        l_scr[...] += p
        s_scr[c] = p
        return carry

    lax.fori_loop(0, n_chunks, pass2, 0)
    inv_l = 1.0 / jnp.maximum(jnp.sum(l_scr[...], axis=-1, keepdims=True), 1e-30)

    def pass3(c, carry):
        k0 = pl.multiple_of(c * ch, ch)
        p = (s_scr[c] * inv_l).astype(MXU_DTYPE)
        acc_scr[...] += jnp.dot(p, vs_ref[pl.ds(k0, ch), :].astype(MXU_DTYPE), preferred_element_type=F32)
        return carry

    lax.fori_loop(0, n_chunks, pass3, 0)
    o_s = acc_scr[...]

    g = jax.nn.sigmoid(gz_ref[...])
    outs = []
    for r in range(R):
        sl = slice(r * bq, (r + 1) * bq)
        outs.append(g[:, 3 * r:3 * r + 1] * o_c[sl] + g[:, 3 * r + 1:3 * r + 2] * o_s[sl]
                    + g[:, 3 * r + 2:3 * r + 3] * o_w[sl])
    o_ref[...] = jnp.concatenate(outs, axis=1).astype(o_ref.dtype)


def _rel_bucket(dist):
    n = jnp.maximum(dist, 0)
    max_exact = N_BUCKETS // 2
    large = max_exact + (jnp.log(jnp.maximum(n, 1).astype(F32) / max_exact)
                         / math.log(MAX_DISTANCE / max_exact) * (N_BUCKETS - max_exact)).astype(jnp.int32)
    large = jnp.minimum(large, N_BUCKETS - 1)
    return jnp.where(n < max_exact, n, large)


def _bias_tiles(tab, dist):
    onehot = (_rel_bucket(dist)[..., None] == jnp.arange(N_BUCKETS)).astype(F32)
    t = jnp.einsum('...b,bgr->...gr', onehot, tab, precision=lax.Precision.HIGHEST)
    nd = dist.ndim
    perm = (nd,) + tuple(range(nd - 2)) + (nd + 1, nd - 2, nd - 1)
    t = jnp.transpose(t, perm)
    return t.reshape(t.shape[:-3] + (t.shape[-3] * t.shape[-2], t.shape[-1]))


def _cover(ncp, nsp):
    c0 = jnp.arange(ncp)[:, None] * CMP_STRIDE
    s0 = jnp.arange(nsp)[None, :] * SEL_LEN
    ov = jnp.minimum(c0 + CMP_LEN, s0 + SEL_LEN) - jnp.maximum(c0, s0)
    return (jnp.maximum(ov, 0).astype(F32) / CMP_LEN).astype(MXU_DTYPE)


def _compress(kv_cmp, cmp_pe, cmp_w1, cmp_w2):
    B, L = kv_cmp.shape[:2]
    r = CMP_LEN // CMP_STRIDE
    nch = L // CMP_STRIDE
    nc = nch - r + 1
    chunks = kv_cmp[:, :nch * CMP_STRIDE].reshape(B, nch, CMP_STRIDE, 2, N_KV, HEAD_DIM)
    blocks = jnp.concatenate([chunks[:, m:m + nc] for m in range(r)], axis=2)
    blocks = blocks + cmp_pe[None, None, :, :, None, :]
    flat = jnp.transpose(blocks, (0, 1, 3, 4, 2, 5)).reshape(B, nc, 2, N_KV, CMP_LEN * HEAD_DIM)
    rnd = lambda v: v.astype(MXU_DTYPE).astype(F32)
    hid = jax.nn.silu(jnp.einsum('bnsgf,sfe->bnsge', rnd(flat), rnd(cmp_w1)))
    comp = jnp.einsum('bnsge,sed->bnsgd', rnd(hid), rnd(cmp_w2))
    return comp[:, :, 0], comp[:, :, 1]


def nsa_prompt(z, B, T, lay, tabs, cmp_pe, cmp_w1, cmp_w2):
    R, G = lay['R'], N_KV
    bq = Q_BLOCK
    nqb = T // bq
    N = B * T
    rows = R * bq
    kv = z[:, lay['okv']:lay['okv'] + 6 * G * HEAD_DIM].reshape(B, T, 6, G, HEAD_DIM)
    kcb, vcb = _compress(kv[:, :, :2], cmp_pe, cmp_w1, cmp_w2)
    nc = kcb.shape[1]
    ncp = tabs['ncp']
    pad = ((0, 0), (0, 0), (0, ncp - nc), (0, 0))
    kcb = jnp.pad(jnp.transpose(kcb, (0, 2, 1, 3)), pad).astype(MXU_DTYPE)
    vcb = jnp.pad(jnp.transpose(vcb, (0, 2, 1, 3)), pad).astype(MXU_DTYPE)
    win = kv[:, :, 4:].reshape(B, T, 2 * G * HEAD_DIM)
    win_pad = jnp.concatenate([jnp.zeros((B, WINDOW, 2 * G * HEAD_DIM), F32), win], axis=1)
    gz = z[:, lay['og']:lay['og'] + 3 * G * R].reshape(N, G, 3 * R)
    gz = jnp.pad(jnp.transpose(gz, (1, 0, 2)), ((0, 0), (0, 0), (0, LANE - 3 * R)))
    ns = -(-T // SEL_LEN)
    nsr = tabs['cover_t'].shape[0]
    ch = SEL_CHUNK if T % SEL_CHUNK == 0 else Q_BLOCK
    qcol = lay['oq'] // (R * HEAD_DIM)
    kcol = lay['okv'] // HEAD_DIM
    wb = WINDOW + bq
    kernel = functools.partial(_nsa_kernel, R=R, bq=bq, ns=ns, n_top=min(SEL_TOP, ns))
    return pl.pallas_call(
        kernel,
        grid=(B, G, nqb),
        in_specs=[
            pl.BlockSpec((bq, R * HEAD_DIM), lambda b, g, i: (b * nqb + i, qcol + g)),
            pl.BlockSpec((None, bq, LANE), lambda b, g, i: (g, b * nqb + i, 0)),
            pl.BlockSpec((None, None, ncp, HEAD_DIM), lambda b, g, i: (b, g, 0, 0)),
            pl.BlockSpec((None, None, ncp, HEAD_DIM), lambda b, g, i: (b, g, 0, 0)),
            pl.BlockSpec((T, HEAD_DIM), lambda b, g, i: (b, kcol + 2 * G + g)),
            pl.BlockSpec((T, HEAD_DIM), lambda b, g, i: (b, kcol + 3 * G + g)),
            pl.BlockSpec((None, T + WINDOW, HEAD_DIM), lambda b, g, i: (b, 0, g)),
            pl.BlockSpec((None, T + WINDOW, HEAD_DIM), lambda b, g, i: (b, 0, G + g)),
            pl.BlockSpec((None, None, rows, ncp), lambda b, g, i: (g, i, 0, 0)),
            pl.BlockSpec((None, 3, rows, Q_BLOCK), lambda b, g, i: (g, 0, 0, 0)),
            pl.BlockSpec((None, rows, wb), lambda b, g, i: (g, 0, 0)),
            pl.BlockSpec((nsr, ncp), lambda b, g, i: (0, 0)),
        ],
        out_specs=pl.BlockSpec((bq, R * HEAD_DIM), lambda b, g, i: (b * nqb + i, g)),
        out_shape=jax.ShapeDtypeStruct((N, G * R * HEAD_DIM), MXU_DTYPE),
        scratch_shapes=[pltpu.VMEM((T // ch, rows, ch), F32), pltpu.VMEM((rows, ch), F32),
                        pltpu.VMEM((rows, ch), F32), pltpu.VMEM((rows, HEAD_DIM), F32)],
        compiler_params=_cparams(("parallel", "parallel", "arbitrary")),
    )(z, gz, kcb, vcb, z, z, win_pad, win_pad, tabs['bias_c'], tabs['bias_s'], tabs['bias_w'], tabs['cover_t'])


def nsa_prompt_tables(rel_bias, T, R):
    G = N_KV
    bq = Q_BLOCK
    nqb = T // bq
    lut = rel_bias.astype(F32).reshape(N_BUCKETS, G, R)
    nc = T // CMP_STRIDE - CMP_LEN // CMP_STRIDE + 1
    ncp = -(-nc // LANE) * LANE
    ns = -(-T // SEL_LEN)
    nsr = -(-ns // SUBLANE) * SUBLANE
    i = jnp.arange(bq)[:, None]
    t = jnp.arange(T).reshape(nqb, bq, 1)
    cmp_end = (jnp.arange(ncp) * CMP_STRIDE + CMP_LEN - 1)[None, None, :]
    bias_c = _bias_tiles(lut, t - cmp_end)
    j = jnp.arange(Q_BLOCK)[None, :]
    d_s = jnp.stack([i - j, Q_BLOCK + i - j, jnp.full((bq, Q_BLOCK), MAX_DISTANCE)])
    bias_s = _bias_tiles(lut, d_s)
    jw = jnp.arange(WINDOW + bq)[None, :]
    bias_w = _bias_tiles(lut, i - jw + WINDOW)
    return dict(bias_c=bias_c, bias_s=bias_s, bias_w=bias_w, cover_t=_cover(ncp, nsr).T, ncp=ncp)


ROWS_PER_TOKEN = 4 * N_KV
CMP_PAGES_PER_STEP = 8
ATT_PAGES_PER_STEP = 4


def _cmp_kernel(pt_ref, *refs, n_pg, chunks_per_page):
    page_refs = refs[:n_pg]
    pe_ref, w1_ref, o_ref, lhs_scr = refs[n_pg], refs[n_pg + 1], refs[n_pg + 2], refs[n_pg + 3]
    cpp = chunks_per_page
    stride = CMP_STRIDE * ROWS_PER_TOKEN
    for k in range(n_pg):
        for s in range(2):
            for g in range(N_KV):
                r0 = g * n_pg * cpp + k * cpp
                for j in range(CMP_STRIDE):
                    piece = page_refs[k][pl.ds(j * ROWS_PER_TOKEN + s * N_KV + g, cpp, stride=stride), :]
                    for h in range(2):
                        lhs_scr[h, s, r0:r0 + cpp, j * HEAD_DIM:(j + 1) * HEAD_DIM] = (
                            piece + pe_ref[h * CMP_STRIDE + j, s:s + 1, :])
    half = CMP_STRIDE * HEAD_DIM
    nrow = n_pg * cpp
    for s in range(2):
        top = jnp.dot(lhs_scr[0, s].astype(MXU_DTYPE), w1_ref[s, :half, :].astype(MXU_DTYPE),
                      preferred_element_type=F32)
        bot = jnp.dot(lhs_scr[1, s].astype(MXU_DTYPE), w1_ref[s, half:, :].astype(MXU_DTYPE),
                      preferred_element_type=F32)
        for g in range(N_KV):
            o_ref[s, 0, g] = top[g * nrow:(g + 1) * nrow]
            o_ref[s, 1, g] = bot[g * nrow:(g + 1) * nrow]


def decode_compress(cache2, page_table, cmp_pe, cmp_w1, layer):
    bs, n_pages = page_table.shape
    prow = cache2.shape[2]
    cpp = prow // ROWS_PER_TOKEN // CMP_STRIDE
    n_pg = math.gcd(CMP_PAGES_PER_STEP, n_pages)
    nch = n_pages * cpp
    page_specs = [pl.BlockSpec((None, None, prow, HEAD_DIM),
                               functools.partial(lambda b, i, pt, k: (layer, pt[b, i * n_pg + k], 0, 0), k=k))
                  for k in range(n_pg)]
    grid_spec = pltpu.PrefetchScalarGridSpec(
        num_scalar_prefetch=1,
        grid=(bs, n_pages // n_pg),
        in_specs=page_specs + [pl.BlockSpec((None, CMP_LEN, 2, HEAD_DIM), lambda b, i, pt: (layer, 0, 0, 0)),
                               pl.BlockSpec((None, 2, CMP_LEN * HEAD_DIM, HEAD_DIM), lambda b, i, pt: (layer, 0, 0, 0))],
        out_specs=pl.BlockSpec((None, 2, 2, N_KV, n_pg * cpp, HEAD_DIM), lambda b, i, pt: (b, 0, 0, 0, i, 0)),
        scratch_shapes=[pltpu.VMEM((2, 2, N_KV * n_pg * cpp, CMP_STRIDE * HEAD_DIM), F32)],
    )
    return pl.pallas_call(
        functools.partial(_cmp_kernel, n_pg=n_pg, chunks_per_page=cpp),
        grid_spec=grid_spec,
        out_shape=jax.ShapeDtypeStruct((bs, 2, 2, N_KV, nch, HEAD_DIM), F32),
        compiler_params=_cparams(("parallel", "arbitrary")),
    )(page_table, *([cache2] * n_pg), cmp_pe, cmp_w1)


def _decode_kernel(pt_ref, z_ref, rt_ref, parts_ref, *refs, lay, T, R, past_len, n_pg, ns, n_top, wb):
    page_refs = refs[:n_pg]
    (win_ref, w2_ref, bc_ref, bs_ref, bw_ref, cov_ref, o_ref, q_scr, new_scr, sel_scr, m_scr, l_scr, acc_scr,
     oc_scr, st_scr) = refs[n_pg:]
    p = pl.program_id(1)
    G = N_KV
    rows = R * T
    scale = HEAD_DIM ** -0.5
    page = page_refs[0].shape[0] // ROWS_PER_TOKEN
    span = n_pg * page
    t_row = rt_ref[...]
    q_pos_row = past_len + t_row
    nsp = cov_ref.shape[1]
    kvw = N_KV * HEAD_DIM

    @pl.when(p == 0)
    def _init():
        z = z_ref[...]
        new_scr[...] = jnp.zeros(new_scr.shape, F32)
        for g in range(G):
            for r in range(R):
                c0 = lay['oq'] + (g * R + r) * HEAD_DIM
                q_scr[g, r * T:(r + 1) * T, :] = z[:, c0:c0 + HEAD_DIM]
            for si, s in enumerate((2, 3, 4, 5)):
                c0 = lay['okv'] + s * kvw + g * HEAD_DIM
                new_scr[si, g, 0:T, :] = z[:, c0:c0 + HEAD_DIM]
        nch = parts_ref.shape[3]
        cmp_end = lax.broadcasted_iota(jnp.int32, (1, nch), 1) * CMP_STRIDE + (CMP_LEN - 1)
        blk = lax.broadcasted_iota(jnp.int32, (rows, nsp), 1)
        blk_f = blk.astype(F32)
        cur = jnp.right_shift(q_pos_row, int(math.log2(SEL_LEN)))
        forced = (blk == 0) | ((blk <= cur) & (blk > cur - N_LOCAL_SEL))
        valid = blk * SEL_LEN <= q_pos_row
        cov = cov_ref[...]
        for g in range(G):
            comp = []
            for s in range(2):
                pre = parts_ref[s, 0, g] + pltpu.roll(parts_ref[s, 1, g], nch - 1, 0)
                hid = pre * jax.nn.sigmoid(pre)
                comp.append(jnp.dot(hid.astype(MXU_DTYPE), w2_ref[s].astype(MXU_DTYPE),
                                    preferred_element_type=F32).astype(MXU_DTYPE))
            qr = q_scr[g].astype(MXU_DTYPE)
            s_c = _dot_nt(qr, comp[0]) * scale + bc_ref[g]
            p_c = _softmax_rows(s_c, cmp_end <= q_pos_row).astype(MXU_DTYPE)
            oc_scr[g] = jnp.dot(p_c, comp[1], preferred_element_type=F32)
            imp = jnp.dot(p_c, cov, preferred_element_type=F32)
            imp_all = imp
            for r in range(1, R):
                imp_all = imp_all + pltpu.roll(imp, r * T, 0)
            score = jnp.where(forced, FORCE_SCORE, jnp.where(valid, imp_all, -1.0))
            score = jnp.where(blk < ns, score, -2.0)
            sel = jnp.zeros((rows, nsp), F32)
            for _ in range(n_top):
                mx = jnp.max(score, axis=-1, keepdims=True)
                first = jnp.min(jnp.where(score == mx, blk_f, float(nsp)), axis=-1, keepdims=True)
                hit = blk_f == first
                sel = jnp.where(hit, 1.0, sel)
                score = jnp.where(hit, -3.0e38, score)
            sel_scr[g] = sel
        m_scr[...] = jnp.full(m_scr.shape, NEG_INF, F32)
        l_scr[...] = jnp.zeros(l_scr.shape, F32)
        acc_scr[...] = jnp.zeros(acc_scr.shape, F32)

    def flash_update(g, s, ok, v):
        s = jnp.where(ok, s, NEG_INF)
        m_old = m_scr[g]
        m_new = jnp.maximum(m_old, jnp.max(s, axis=-1, keepdims=True))
        alpha = jnp.exp(m_old - m_new)
        pe = jnp.where(ok, jnp.exp(s - m_new), 0.0)
        l_scr[g] = alpha * l_scr[g] + jnp.sum(pe, axis=-1, keepdims=True)
        acc_scr[g] = alpha * acc_scr[g] + jnp.dot(pe.astype(MXU_DTYPE), v, preferred_element_type=F32)
        m_scr[g] = m_new

    e_row = lax.broadcasted_iota(jnp.int32, (nsp, span), 0)
    e_col = jnp.right_shift(lax.broadcasted_iota(jnp.int32, (nsp, span), 1), int(math.log2(SEL_LEN)))
    expand = (e_row == p * (span // SEL_LEN) + e_col).astype(MXU_DTYPE)
    last = p == pl.num_programs(1) - 1
    for g in range(G):
        k = jnp.concatenate([pr[pl.ds(2 * N_KV + g, page, stride=ROWS_PER_TOKEN), :] for pr in page_refs],
                            axis=0).astype(MXU_DTYPE)
        v = jnp.concatenate([pr[pl.ds(3 * N_KV + g, page, stride=ROWS_PER_TOKEN), :] for pr in page_refs],
                            axis=0).astype(MXU_DTYPE)
        qr = q_scr[g].astype(MXU_DTYPE)
        far = [bs_ref[g, 2]] * (n_pg - 1)
        bias = jnp.where(last, jnp.concatenate(far + [bs_ref[g, 1]], axis=1),
                         jnp.concatenate(far + [bs_ref[g, 2]], axis=1))
        s = _dot_nt(qr, k) * scale + bias
        picked = jnp.dot(sel_scr[g].astype(MXU_DTYPE), expand, preferred_element_type=F32)
        flash_update(g, s, picked > 0.5, v)

    @pl.when(last)
    def _finish():
        z = z_ref[...]
        j_new = lax.broadcasted_iota(jnp.int32, (1, new_scr.shape[2]), 1)
        ok_new = (j_new <= t_row) & (j_new < T)
        j_w = lax.broadcasted_iota(jnp.int32, (1, wb), 1)
        d_w = wb + t_row - j_w
        ok_w = (d_w >= 0) & (d_w < WINDOW) & (past_len - wb + j_w >= 0)
        gates = jax.nn.sigmoid(z[:, lay['og']:lay['og'] + LANE])
        for g in range(G):
            qr = q_scr[g].astype(MXU_DTYPE)
            s_n = _dot_nt(qr, new_scr[0, g].astype(MXU_DTYPE)) * scale + bs_ref[g, 0]
            flash_update(g, s_n, ok_new, new_scr[1, g].astype(MXU_DTYPE))
            o_s = acc_scr[g] / jnp.maximum(l_scr[g], 1e-30)
            kw = win_ref[pl.ds(g, wb, stride=2 * N_KV), :].astype(MXU_DTYPE)
            vw = win_ref[pl.ds(N_KV + g, wb, stride=2 * N_KV), :].astype(MXU_DTYPE)
            s1 = jnp.where(ok_w, _dot_nt(qr, kw) * scale + bw_ref[g], NEG_INF)
            s2 = jnp.where(ok_new, _dot_nt(qr, new_scr[2, g].astype(MXU_DTYPE)) * scale + bs_ref[g, 0], NEG_INF)
            mw = jnp.maximum(jnp.max(s1, axis=-1, keepdims=True), jnp.max(s2, axis=-1, keepdims=True))
            e1 = jnp.where(ok_w, jnp.exp(s1 - mw), 0.0)
            e2 = jnp.where(ok_new, jnp.exp(s2 - mw), 0.0)
            den = jnp.sum(e1, axis=-1, keepdims=True) + jnp.sum(e2, axis=-1, keepdims=True)
            o_w = (jnp.dot(e1.astype(MXU_DTYPE), vw, preferred_element_type=F32)
                   + jnp.dot(e2.astype(MXU_DTYPE), new_scr[3, g].astype(MXU_DTYPE), preferred_element_type=F32))
            o_w = o_w / jnp.maximum(den, 1e-30)
            st_scr[0] = oc_scr[g]
            st_scr[1] = o_s
            st_scr[2] = o_w
            for r in range(R):
                gc = (g * R + r) * 3
                rs = pl.ds(r * T, T)
                out = (gates[:, gc:gc + 1] * st_scr[0, rs, :] + gates[:, gc + 1:gc + 2] * st_scr[1, rs, :]
                       + gates[:, gc + 2:gc + 3] * st_scr[2, rs, :])
                c0 = (g * R + r) * HEAD_DIM
                o_ref[:, c0:c0 + HEAD_DIM] = out


def nsa_decode(zs3, cache2, page_table, win2, lay, tabs, cmp_pe, cmp_w1, cmp_w2, layer, past_len):
    bs, T, NZ = zs3.shape
    R, G = lay['R'], N_KV
    rows = R * T
    n_pages = page_table.shape[1]
    prow = cache2.shape[2]
    page = prow // ROWS_PER_TOKEN
    assert past_len % page == 0 and page % SEL_LEN == 0 and T < CMP_STRIDE and T <= SEL_LEN
    parts = decode_compress(cache2, page_table, cmp_pe, cmp_w1, layer)
    nch = parts.shape[4]
    wb = win2.shape[2] // (2 * N_KV)
    ns = -(-(past_len + T) // SEL_LEN)
    nsp = tabs['cover'].shape[1]
    npad = tabs['bias_s'].shape[-1]
    n_pg = math.gcd(ATT_PAGES_PER_STEP, n_pages)
    kernel = functools.partial(_decode_kernel, lay=lay, T=T, R=R, past_len=past_len, n_pg=n_pg, ns=ns,
                               n_top=min(SEL_TOP, ns), wb=wb)
    page_specs = [pl.BlockSpec((None, None, prow, HEAD_DIM),
                               functools.partial(lambda b, p, pt, k: (layer, pt[b, p * n_pg + k], 0, 0), k=k))
                  for k in range(n_pg)]
    grid_spec = pltpu.PrefetchScalarGridSpec(
        num_scalar_prefetch=1,
        grid=(bs, n_pages // n_pg),
        in_specs=[
            pl.BlockSpec((None, T, NZ), lambda b, p, pt: (b, 0, 0)),
            pl.BlockSpec((rows, 1), lambda b, p, pt: (0, 0)),
            pl.BlockSpec((None, 2, 2, G, nch, HEAD_DIM), lambda b, p, pt: (b, 0, 0, 0, 0, 0))]
        + page_specs + [
            pl.BlockSpec((None, None, wb * 2 * N_KV, HEAD_DIM), lambda b, p, pt: (layer, b, 0, 0)),
            pl.BlockSpec((None, 2, HEAD_DIM, HEAD_DIM), lambda b, p, pt: (layer, 0, 0, 0)),
            pl.BlockSpec((G, rows, nch), lambda b, p, pt: (0, 0, 0)),
            pl.BlockSpec((G, 3, rows, npad), lambda b, p, pt: (0, 0, 0, 0)),
            pl.BlockSpec((G, rows, wb), lambda b, p, pt: (0, 0, 0)),
            pl.BlockSpec((nch, nsp), lambda b, p, pt: (0, 0)),
        ],
        out_specs=pl.BlockSpec((None, T, G * R * HEAD_DIM), lambda b, p, pt: (b, 0, 0)),
        scratch_shapes=[pltpu.VMEM((G, rows, HEAD_DIM), F32), pltpu.VMEM((4, G, npad, HEAD_DIM), F32),
                        pltpu.VMEM((G, rows, nsp), F32), pltpu.VMEM((G, rows, 1), F32),
                        pltpu.VMEM((G, rows, 1), F32), pltpu.VMEM((G, rows, HEAD_DIM), F32),
                        pltpu.VMEM((G, rows, HEAD_DIM), F32), pltpu.VMEM((3, rows, HEAD_DIM), F32)],
    )
    return pl.pallas_call(
        kernel,
        grid_spec=grid_spec,
        out_shape=jax.ShapeDtypeStruct((bs, T, G * R * HEAD_DIM), F32),
        compiler_params=_cparams(("parallel", "arbitrary")),
    )(page_table, zs3, tabs['row_t'], parts, *([cache2] * n_pg), win2, cmp_w2, tabs['bias_c'], tabs['bias_s'],
      tabs['bias_w'], tabs['cover'])


def nsa_decode_tables(rel_bias, T, R, past_len, wb, page):
    G = N_KV
    tab = rel_bias.astype(F32).reshape(N_BUCKETS, G, R)
    nch = past_len // CMP_STRIDE
    ns = -(-(past_len + T) // SEL_LEN)
    nsp = -(-ns // LANE) * LANE
    t = jnp.arange(T)[:, None]
    cmp_end = (jnp.arange(nch) * CMP_STRIDE + CMP_LEN - 1)[None, :]
    bias_c = _bias_tiles(tab, past_len + t - cmp_end)
    j = jnp.arange(page)[None, :]
    d_s = jnp.stack([t - j, page + t - j, jnp.full((T, page), MAX_DISTANCE)])
    bias_s = _bias_tiles(tab, d_s)
    jw = jnp.arange(wb)[None, :]
    bias_w = _bias_tiles(tab, wb + t - jw)
    row_t = jnp.tile(jnp.arange(T, dtype=jnp.int32), R).reshape(R * T, 1)
    return dict(bias_c=bias_c, bias_s=bias_s, bias_w=bias_w, cover=_cover(nch, nsp), row_t=row_t)


CONV_HIST = 32


def _conv_mix_kernel(ab_ref, ac_ref, ah_ref, bl_ref, bg_ref, zc_ref, cap_ref, cbp_ref, pp_ref, caw_ref, cbw_ref,
                     cbb_ref, lng_ref, lnb_ref, pw_ref, ps_ref, y_ref, cao_ref, cbo_ref, po_ref,
                     ea_scr, eb_scr, ec_scr, ear_scr, ebr_scr, *, tt, pos0):
    i = pl.program_id(1)
    H = CONV_HIST
    ka, kb, kp = caw_ref.shape[0] - 1, cbw_ref.shape[0] - 1, pp_ref.shape[0]
    da, db = ab_ref.shape[1], bl_ref.shape[1]
    rnd = lambda v: v.astype(MXU_DTYPE).astype(F32)

    @pl.when(i == 0)
    def _():
        for scr, prev, k in ((ea_scr, cap_ref, ka), (eb_scr, cbp_ref, kb), (ec_scr, pp_ref, kp)):
            scr[0:H, :] = jnp.zeros((H, scr.shape[1]), F32)
            scr[H - k:H, :] = prev[...]
        for scr, prev, k in ((ear_scr, cap_ref, ka), (ebr_scr, cbp_ref, kb)):
            scr[0:H, :] = jnp.zeros((H, scr.shape[1]), F32)
            scr[H - k:H, :] = rnd(prev[...])

    ua = ac_ref[...] * ah_ref[...]
    ea_scr[H:H + tt, :] = ua
    ear_scr[H:H + tt, :] = rnd(ua)
    conv = None
    for k in range(ka + 1):
        term = rnd(caw_ref[k:k + 1, :]) * ear_scr[pl.ds(H - ka + k, tt), :]
        conv = term if conv is None else conv + term
    y_ref[:, 0:da] = (ab_ref[...] * conv).astype(y_ref.dtype)

    ub = bl_ref[...] * jax.nn.sigmoid(bg_ref[...])
    eb_scr[H:H + tt, :] = ub
    ebr_scr[H:H + tt, :] = rnd(ub)
    cb = None
    for k in range(kb + 1):
        term = rnd(cbw_ref[k:k + 1, :]) * ebr_scr[pl.ds(H - kb + k, tt), :]
        cb = term if cb is None else cb + term
    cb = cb + cbb_ref[...]
    mu = jnp.mean(cb, axis=-1, keepdims=True)
    var = jnp.mean(jnp.square(cb - mu), axis=-1, keepdims=True)
    ln = (cb - mu) * lax.rsqrt(var + LN_EPS) * lng_ref[...] + lnb_ref[...]
    y_ref[:, da:da + db] = (ln * jax.nn.sigmoid(ln)).astype(y_ref.dtype)

    zc = zc_ref[...]
    ec_scr[H:H + tt, :] = zc
    pos = pos0 + i * tt + lax.broadcasted_iota(jnp.int32, (tt, 1), 0)
    cg = zc.shape[1] // len(POOL_WINDOWS)
    for gi, w in enumerate(POOL_WINDOWS):
        c0 = gi * cg
        ssum = None
        for k in range(w):
            term = ec_scr[pl.ds(H - k, tt), c0:c0 + cg]
            ssum = term if ssum is None else ssum + term
        cnt = jnp.minimum(w, pos + 1).astype(F32)
        pooled = ssum / cnt - zc[:, c0:c0 + cg]
        mixed = jnp.dot(pooled.astype(MXU_DTYPE), pw_ref[gi].astype(MXU_DTYPE), preferred_element_type=F32)
        y_ref[:, da + db + c0:da + db + c0 + cg] = (mixed * ps_ref[:, c0:c0 + cg]).astype(y_ref.dtype)

    for scr in (ea_scr, eb_scr, ec_scr, ear_scr, ebr_scr):
        tail = scr[pl.ds(tt, H), :]
        scr[0:H, :] = tail

    @pl.when(i == pl.num_programs(1) - 1)
    def _():
        cao_ref[...] = ea_scr[H - ka:H, :]
        cbo_ref[...] = eb_scr[H - kb:H, :]
        po_ref[...] = ec_scr[H - kp:H, :]


def conv_mix(z3, ca_prev, cb_prev, pool_prev, prm, lay, layer, pos0, out_dtype):
    B, T, _ = z3.shape
    DA, DB, DC = lay['DA'], lay['DB'], lay['DC']
    assert DA == DB == DC and lay['ob'] == 3 * DA and lay['oc'] == 5 * DA
    tt = _pick(T, (256, 128, 64, 32)) if T >= CONV_HIST else T
    nt = T // tt
    ka, kb, kp = ca_prev.shape[1], cb_prev.shape[1], pool_prev.shape[1]
    ngrp = len(POOL_WINDOWS)
    zcol = lambda c: pl.BlockSpec((None, tt, DA), lambda b, i: (b, i, c))
    seq = lambda k, w: pl.BlockSpec((None, k, w), lambda b, i: (b, 0, 0))
    lyr = lambda k, w: pl.BlockSpec((None, k, w), lambda b, i: (layer, 0, 0))
    vec = lambda a: a.reshape(a.shape[0], 1, a.shape[1])
    return pl.pallas_call(
        functools.partial(_conv_mix_kernel, tt=tt, pos0=pos0),
        grid=(B, nt),
        in_specs=[zcol(0), zcol(1), zcol(2), zcol(3), zcol(4), zcol(5),
                  seq(ka, DA), seq(kb, DB), seq(kp, DC),
                  lyr(ka + 1, DA), lyr(kb + 1, DB), lyr(1, DB), lyr(1, DB), lyr(1, DB),
                  pl.BlockSpec((None, ngrp, DC // ngrp, DC // ngrp), lambda b, i: (layer, 0, 0, 0)), lyr(1, DC)],
        out_specs=[pl.BlockSpec((None, tt, DA + DB + DC), lambda b, i: (b, i, 0)),
                   seq(ka, DA), seq(kb, DB), seq(kp, DC)],
        out_shape=[jax.ShapeDtypeStruct((B, T, DA + DB + DC), out_dtype),
                   jax.ShapeDtypeStruct((B, ka, DA), F32), jax.ShapeDtypeStruct((B, kb, DB), F32),
                   jax.ShapeDtypeStruct((B, kp, DC), F32)],
        scratch_shapes=[pltpu.VMEM((CONV_HIST + tt, DA), F32), pltpu.VMEM((CONV_HIST + tt, DB), F32),
                        pltpu.VMEM((CONV_HIST + tt, DC), F32), pltpu.VMEM((CONV_HIST + tt, DA), F32),
                        pltpu.VMEM((CONV_HIST + tt, DB), F32)],
        compiler_params=_cparams(("parallel", "arbitrary")),
    )(z3, z3, z3, z3, z3, z3, ca_prev, cb_prev, pool_prev, prm['conv_a_w'], prm['conv_b_w'],
      vec(prm['conv_b_b']), vec(prm['ln_b_g']), vec(prm['ln_b_b']), prm['pool_w'], vec(prm['pool_scale']))


def _layout(prm, D):
    DA = prm['conv_a_w'].shape[-1]
    DB = prm['conv_b_w'].shape[-1]
    DC = prm['pool_scale'].shape[-1]
    NH = prm['rel_bias'].shape[1]
    R = NH // N_KV
    DD = NH * HEAD_DIM
    ob = 3 * DA
    oc = ob + 2 * DB
    oq = oc + DC
    okv = oq + DD
    og = okv + 6 * N_KV * HEAD_DIM
    assert oq % (R * HEAD_DIM) == 0 and okv % HEAD_DIM == 0
    return dict(DA=DA, DB=DB, DC=DC, R=R, DD=DD, ob=ob, oc=oc, oq=oq, okv=okv, og=og, n_in=og + 3 * NH)


def _mixer(z, B, T, l, pos0, prev, win_keep, prm, lay, tabs):
    kv_past, win_prev, ca_prev, cb_prev, pool_prev = prev
    z3 = z.reshape(B, T, z.shape[-1])
    small = T < 2 * SUBLANE
    y_abc, ca, cb, pool = conv_mix(z3, ca_prev, cb_prev, pool_prev, prm, lay, l, pos0, F32 if small else MXU_DTYPE)
    kv = z3[..., lay['okv']:lay['okv'] + 6 * N_KV * HEAD_DIM].reshape(B, T, 6, N_KV, HEAD_DIM)
    win_rows = jnp.concatenate([win_prev, kv[:, :, 4:]], axis=1)
    if kv_past is None:
        y_d = nsa_prompt(z, B, T, lay, tabs, prm['cmp_pe'][l], prm['cmp_w1'][l], prm['cmp_w2'][l])
    else:
        y_d = nsa_decode(z3, kv_past['cache2'], kv_past['page_table'], kv_past['win2'], lay, tabs,
                         prm['cmp_pe'], prm['cmp_w1'], prm['cmp_w2'], l, pos0)
    y_abc = y_abc.reshape(B * T, -1).astype(MXU_DTYPE)
    y_d = y_d.reshape(B * T, -1).astype(MXU_DTYPE)
    state = (kv[:, :, :4], win_rows[:, -win_keep:], ca, cb, pool)
    return (y_abc, y_d), state


class _Group:
    def __init__(self, x3, mods, pos0, prev_fn, win_keep, tabs, prm, lay):
        self.B, self.T, self.D = x3.shape
        self.mods, self.pos0, self.prev_fn, self.win_keep, self.tabs = mods, pos0, prev_fn, win_keep, tabs
        self.prm, self.lay = prm, lay
        N = self.B * self.T
        self.tr = _pick(N, (256, 128, 64, 32, 16, 8))
        if self.T % self.tr == 0:
            self.per_seq = lambda m: m[:, None, :]
        else:
            assert self.tr == N
            self.per_seq = lambda m: jnp.repeat(m, self.T, axis=0)[None]
        splits, acc = [], 0
        for wdt in (lay['DA'], lay['DB'], lay['DC'], lay['DD']):
            splits.append((acc, acc + wdt))
            acc += wdt
        self.splits = tuple(splits)
        self.outs = ([], [], [], [], [])
        self.x = x3.reshape(N, self.D)
        mod = mods[0]
        self.h = rms_mod(self.x, prm['norm_w'][0, 0][None], self.per_seq(1.0 + mod[:, 1]), self.per_seq(mod[:, 0]),
                         self.T, self.tr)

    def mixer_sublayer(self, l):
        prm, nw, mod, ps = self.prm, self.prm['norm_w'], self.mods[l], self.per_seq
        z = matmul(self.h, prm['w_in'], l)
        (y_abc, y_d), st = _mixer(z, self.B, self.T, l, self.pos0, self.prev_fn(l), self.win_keep, prm, self.lay,
                                  self.tabs)
        merged = merge(self.h, y_abc, y_d, prm['w_mgate'], prm['w_branch'], l, self.splits)
        y = matmul(merged, prm['w_out'], l)
        self.x, h2 = resid_norm(self.x, y, nw[l, 1][None], ps(mod[:, 2]), nw[l, 2][None],
                                ps(1.0 + mod[:, 4]), ps(mod[:, 3]), self.T, self.tr)
        for a, s in zip(self.outs, st):
            a.append(s)
        return h2

    def ffn_residual(self, l, f):
        nw, mod, ps = self.prm['norm_w'], self.mods[l], self.per_seq
        ln = min(l + 1, nw.shape[0] - 1)
        modn = self.mods[ln]
        self.x, self.h = resid_norm(self.x, f, nw[l, 3][None], ps(mod[:, 5]), nw[ln, 0][None],
                                    ps(1.0 + modn[:, 1]), ps(modn[:, 0]), self.T, self.tr)

    def result(self):
        return self.x.reshape(self.B, self.T, self.D), [jnp.stack(a) for a in self.outs]


def _run_trunks(groups, prm):
    for l in range(prm['norm_w'].shape[0]):
        h2s = [g.mixer_sublayer(l) for g in groups]
        routes = [moe_route_shared(h2, prm, l) for h2 in h2s]
        routed = moe_routed(h2s, [r[0] for r in routes], [r[1] for r in routes], prm, l)
        for g, r, rt in zip(groups, routes, routed):
            g.ffn_residual(l, rt + r[2])
    return [g.result() for g in groups]


def kernel(x_prompt, x_sample, cache_kv, state_kv_win, state_conv_a, state_conv_b, state_pool, page_table,
           c_prompt, c_sample, rel_bias, ada_w, ada_b, norm_w, w_in, conv_a_w, conv_b_w, conv_b_b, ln_b_g, ln_b_b,
           pool_w, pool_scale, cmp_pe, cmp_w1, cmp_w2, w_branch, w_mgate, w_out, w_router, b_router,
           w_exp_gate, w_exp_up, w_exp_down, w_sh_gate, w_sh_up, w_sh_down):
    bp, tp, D = x_prompt.shape
    bs, ts, _ = x_sample.shape
    depth = norm_w.shape[0]
    dt = x_prompt.dtype
    past_len = page_table.shape[1] * cache_kv.shape[2]
    E = b_router.shape[-1]

    prm = dict(rel_bias=rel_bias, norm_w=norm_w, conv_a_w=conv_a_w, conv_b_w=conv_b_w, conv_b_b=conv_b_b,
               ln_b_g=ln_b_g, ln_b_b=ln_b_b, pool_w=pool_w, pool_scale=pool_scale, cmp_pe=cmp_pe, cmp_w1=cmp_w1,
               cmp_w2=cmp_w2, b_router=b_router)
    lay = _layout(prm, D)
    n_in = lay['n_in']
    nz = lay['og'] + 2 * LANE
    prm['w_in'] = jnp.pad(w_in, ((0, 0), (0, 0), (0, nz - n_in))).astype(MXU_DTYPE)
    prm['w_mgate'] = w_mgate.astype(MXU_DTYPE)
    prm['w_branch'] = w_branch.astype(MXU_DTYPE)
    prm['w_out'] = w_out.astype(MXU_DTYPE)
    prm['w_router'] = jnp.pad(w_router, ((0, 0), (0, 0), (0, LANE - E))).astype(MXU_DTYPE)
    prm['w_exp_gate'] = w_exp_gate.astype(MXU_DTYPE)
    prm['w_exp_up'] = w_exp_up.astype(MXU_DTYPE)
    prm['w_exp_down'] = w_exp_down.astype(MXU_DTYPE)
    prm['w_sh_gu'] = jnp.concatenate([w_sh_gate, w_sh_up], axis=-1).astype(MXU_DTYPE)
    prm['w_sh_down'] = w_sh_down.astype(MXU_DTYPE)

    nb = bp + bs
    nbp = -(-nb // 16) * 16
    cs = jnp.pad(jax.nn.silu(jnp.concatenate([c_prompt, c_sample], axis=0)), ((0, nbp - nb), (0, 0)))
    cs = cs.astype(MXU_DTYPE)
    mods_p, mods_s = [], []
    for l in range(depth):
        m = (matmul(cs, ada_w, l, tn=_pick(6 * D, (512, 256, 128)))[:nb] + ada_b[l]).reshape(nb, 6, D)
        mods_p.append(m[:bp])
        mods_s.append(m[bp:])

    tabs = nsa_prompt_tables(rel_bias, tp, lay['R'])

    def prompt_prev(l):
        return (None,
                jnp.zeros((bp, WINDOW, 2, N_KV, HEAD_DIM), dt),
                jnp.zeros((bp, conv_a_w.shape[1] - 1, lay['DA']), dt),
                jnp.zeros((bp, conv_b_w.shape[1] - 1, lay['DB']), dt),
                jnp.zeros((bp, state_pool.shape[2], lay['DC']), dt))

    page = cache_kv.shape[2]
    wbuf = state_kv_win.shape[2]
    paged = dict(cache2=cache_kv.reshape(depth, cache_kv.shape[1], page * ROWS_PER_TOKEN, HEAD_DIM),
                 win2=state_kv_win.reshape(depth, bs, wbuf * 2 * N_KV, HEAD_DIM), page_table=page_table)
    tabs_s = nsa_decode_tables(rel_bias, ts, lay['R'], past_len, wbuf, page)

    def sample_prev(l):
        return (paged, state_kv_win[l], state_conv_a[l], state_conv_b[l], state_pool[l])

    groups = [_Group(x_prompt, mods_p, 0, prompt_prev, min(WINDOW, tp), tabs, prm, lay),
              _Group(x_sample, mods_s, past_len, sample_prev, wbuf, tabs_s, prm, lay)]
    (y_prompt, st_p), (y_sample, st_s) = _run_trunks(groups, prm)
    kv_p, win_p, ca_p, cb_p, pl_p = st_p
    kv_s, win_s, ca_s, cb_s, pl_s = st_s
    return (y_prompt, y_sample, kv_p, kv_s, win_p, win_s, ca_p, ca_s, cb_p, cb_s, pl_p, pl_s)
```

```python
import functools
import math

import jax
import jax.numpy as jnp
from jax import lax
from jax.experimental import pallas as pl
from jax.experimental.pallas import tpu as pltpu

F32 = jnp.float32
MXU_DTYPE = jnp.bfloat16
VMEM_LIMIT_BYTES = 56 * 1024 * 1024
LANE = 128
SUBLANE = 8

RMS_EPS = 1e-6
LN_EPS = 1e-5
N_KV = 4
HEAD_DIM = 128
POOL_WINDOWS = (2, 4, 8, 16)
CMP_LEN = 32
CMP_STRIDE = 16
SEL_LEN = 64
SEL_TOP = 16
N_LOCAL_SEL = 2
WINDOW = 512
Q_BLOCK = 128
SEL_CHUNK = 256
FORCE_SCORE = 1e4
NEG_INF = -1e30
N_BUCKETS = 32
MAX_DISTANCE = 128
TOP_K = 8
MOE_TILE_ROWS = 256


def _pick(n, cands):
    for c in cands:
        if n % c == 0:
            return c
    return n


def _cparams(sem):
    return pltpu.CompilerParams(dimension_semantics=sem, vmem_limit_bytes=VMEM_LIMIT_BYTES)


def _mm_kernel(a_ref, w_ref, o_ref):
    w = w_ref[...].astype(MXU_DTYPE)
    o_ref[...] = jnp.dot(a_ref[...], w, preferred_element_type=F32).astype(o_ref.dtype)


def matmul(a, w, layer=None, out_dtype=F32, tm=None, tn=None):
    M, K = a.shape
    N = w.shape[-1]
    tm = tm or _pick(M, (1024, 512, 256, 128, 64, 32, 16, 8))
    tn = tn or _pick(N, (512, 768, 384, 256, 128))
    if w.ndim == 3:
        w_spec = pl.BlockSpec((None, K, tn), lambda i, j: (layer, 0, j))
    else:
        w_spec = pl.BlockSpec((K, tn), lambda i, j: (0, j))
    return pl.pallas_call(
        _mm_kernel,
        grid=(M // tm, N // tn),
        in_specs=[pl.BlockSpec((tm, K), lambda i, j: (i, 0)), w_spec],
        out_specs=pl.BlockSpec((tm, tn), lambda i, j: (i, j)),
        out_shape=jax.ShapeDtypeStruct((M, N), out_dtype),
        compiler_params=_cparams(("parallel", "parallel")),
    )(a, w)


def _rms(x, w):
    return x * lax.rsqrt(jnp.mean(x * x, axis=-1, keepdims=True) + RMS_EPS) * w


def _rms_mod_kernel(x_ref, nw_ref, sc_ref, sh_ref, h_ref):
    h = _rms(x_ref[...], nw_ref[...]) * sc_ref[0] + sh_ref[0]
    h_ref[...] = h.astype(h_ref.dtype)


def _mod_spec(mod, rows_per_seq, tr):
    Bm, Rm, D = mod.shape
    if Rm == 1:
        per = rows_per_seq // tr
        return pl.BlockSpec((1, 1, D), lambda i: (i // per, 0, 0))
    return pl.BlockSpec((1, Rm, D), lambda i: (i, 0, 0))


def rms_mod(x, nw, sc, sh, rows_per_seq, tr):
    M, D = x.shape
    return pl.pallas_call(
        _rms_mod_kernel,
        grid=(M // tr,),
        in_specs=[pl.BlockSpec((tr, D), lambda i: (i, 0)), pl.BlockSpec((1, D), lambda i: (0, 0)),
                  _mod_spec(sc, rows_per_seq, tr), _mod_spec(sh, rows_per_seq, tr)],
        out_specs=pl.BlockSpec((tr, D), lambda i: (i, 0)),
        out_shape=jax.ShapeDtypeStruct((M, D), MXU_DTYPE),
        compiler_params=_cparams(("parallel",)),
    )(x, nw, sc, sh)


def _resid_kernel(x_ref, y_ref, nwy_ref, gate_ref, nwn_ref, sc_ref, sh_ref, xo_ref, h_ref):
    x = x_ref[...] + gate_ref[0] * _rms(y_ref[...], nwy_ref[...])
    xo_ref[...] = x
    h_ref[...] = (_rms(x, nwn_ref[...]) * sc_ref[0] + sh_ref[0]).astype(h_ref.dtype)


def resid_norm(x, y, nwy, gate, nwn, sc, sh, rows_per_seq, tr):
    M, D = x.shape
    row = pl.BlockSpec((tr, D), lambda i: (i, 0))
    vec = pl.BlockSpec((1, D), lambda i: (0, 0))
    return pl.pallas_call(
        _resid_kernel,
        grid=(M // tr,),
        in_specs=[row, row, vec, _mod_spec(gate, rows_per_seq, tr), vec,
                  _mod_spec(sc, rows_per_seq, tr), _mod_spec(sh, rows_per_seq, tr)],
        out_specs=[row, row],
        out_shape=[jax.ShapeDtypeStruct((M, D), F32), jax.ShapeDtypeStruct((M, D), MXU_DTYPE)],
        compiler_params=_cparams(("parallel",)),
    )(x, y, nwy, gate, nwn, sc, sh)


def _merge_kernel(h_ref, y1_ref, y2_ref, wg0, wg1, wg2, wg3, wb_ref, o_ref, *, splits):
    h = h_ref[...]
    c1 = y1_ref.shape[1]
    acc = None
    for (lo, hi), wg in zip(splits, (wg0, wg1, wg2, wg3)):
        gate = jax.nn.sigmoid(jnp.dot(h, wg[...], preferred_element_type=F32))
        y = y1_ref[:, lo:hi] if hi <= c1 else y2_ref[:, lo - c1:hi - c1]
        br = jnp.dot(y, wb_ref[lo:hi, :], preferred_element_type=F32)
        acc = gate * br if acc is None else acc + gate * br
    o_ref[...] = acc.astype(o_ref.dtype)


def merge(h, y1, y2, w_mgate, w_branch, layer, splits):
    M, D = h.shape
    C1, C2 = y1.shape[1], y2.shape[1]
    tm = _pick(M, (512, 256, 128, 64, 32, 16, 8))
    tn = _pick(D, (256, 128))
    nj = D // tn
    wg_specs = [pl.BlockSpec((None, D, tn), functools.partial(lambda i, j, k: (layer, 0, k * nj + j), k=k))
                for k in range(4)]
    return pl.pallas_call(
        functools.partial(_merge_kernel, splits=splits),
        grid=(M // tm, nj),
        in_specs=[pl.BlockSpec((tm, D), lambda i, j: (i, 0)), pl.BlockSpec((tm, C1), lambda i, j: (i, 0)),
                  pl.BlockSpec((tm, C2), lambda i, j: (i, 0))]
        + wg_specs + [pl.BlockSpec((None, C1 + C2, tn), lambda i, j: (layer, 0, j))],
        out_specs=pl.BlockSpec((tm, tn), lambda i, j: (i, j)),
        out_shape=jax.ShapeDtypeStruct((M, D), MXU_DTYPE),
        compiler_params=_cparams(("parallel", "parallel")),
    )(h, y1, y2, w_mgate, w_mgate, w_mgate, w_mgate, w_branch)


def _moe_kernel(pt_ref, pe_ref, lo_ref, hi_ref, first_ref, np_ref, x_ref, g_ref, wg_ref, wu_ref, wd_ref, o_ref):
    p = pl.program_id(0)

    @pl.when(p < np_ref[0])
    def _():
        x = x_ref[...]
        gp = jnp.dot(x, wg_ref[...], preferred_element_type=F32)
        up = jnp.dot(x, wu_ref[...], preferred_element_type=F32)
        hid = (gp * jax.nn.sigmoid(gp)) * up * g_ref[...]
        y = jnp.dot(hid.astype(MXU_DTYPE), wd_ref[...], preferred_element_type=F32)
        row = lax.broadcasted_iota(jnp.int32, (x.shape[0], 1), 0)
        mine = (row >= lo_ref[p]) & (row < hi_ref[p])

        @pl.when(first_ref[p] == 1)
        def _():
            o_ref[...] = jnp.where(mine, y, 0.0)

        @pl.when(first_ref[p] == 0)
        def _():
            o_ref[...] = jnp.where(mine, y, o_ref[...])


def moe_grouped(x_sorted, gate_sorted, pairs, w_eg, w_eu, w_ed, layer, tm):
    A, D = x_sorted.shape
    Fe = w_eg.shape[-1]
    pair_tile, pair_expert, lo, hi, first, n_pairs = pairs
    row_spec = lambda w: pl.BlockSpec((tm, w), lambda p, pt, pe, lo, hi, fi, npr: (pt[p], 0))
    w_spec = lambda a, b: pl.BlockSpec((None, None, a, b), lambda p, pt, pe, lo, hi, fi, npr: (layer, pe[p], 0, 0))
    grid_spec = pltpu.PrefetchScalarGridSpec(
        num_scalar_prefetch=6,
        grid=(pair_tile.shape[0],),
        in_specs=[row_spec(D), row_spec(1), w_spec(D, Fe), w_spec(D, Fe), w_spec(Fe, D)],
        out_specs=row_spec(D),
    )
    return pl.pallas_call(
        _moe_kernel,
        grid_spec=grid_spec,
        out_shape=jax.ShapeDtypeStruct((A, D), F32),
        compiler_params=_cparams(("arbitrary",)),
    )(pair_tile, pair_expert, lo, hi, first, n_pairs, x_sorted, gate_sorted, w_eg, w_eu, w_ed)


def moe_route_shared(h2, prm, layer):
    E = prm['b_router'].shape[-1]
    logits = matmul(h2, prm['w_router'], layer)[:, :E]
    aff = jax.nn.sigmoid(logits)
    _, idx = lax.top_k(aff + prm['b_router'][layer].astype(F32), TOP_K)
    top_aff = jnp.take_along_axis(aff, idx, axis=-1)
    wts = top_aff / jnp.sum(top_aff, axis=-1, keepdims=True)
    gu = matmul(h2, prm['w_sh_gu'], layer)
    Fs = gu.shape[1] // 2
    hs = (jax.nn.silu(gu[:, :Fs]) * gu[:, Fs:]).astype(MXU_DTYPE)
    shared = matmul(hs, prm['w_sh_down'], layer)
    return idx, wts, shared


def moe_routed(h2s, idxs, wtss, prm, layer):
    sizes = [h.shape[0] for h in h2s]
    h2 = jnp.concatenate(h2s, axis=0)
    idx = jnp.concatenate(idxs, axis=0)
    wts = jnp.concatenate(wtss, axis=0)
    N = h2.shape[0]
    E = prm['b_router'].shape[-1]
    tm = MOE_TILE_ROWS
    A = N * TOP_K
    A_pad = -(-A // tm) * tm
    i32 = jnp.int32
    e_flat = jnp.pad(idx.reshape(A).astype(i32), (0, A_pad - A), constant_values=E - 1)
    w_flat = jnp.pad(wts.reshape(A), (0, A_pad - A))
    a_ids = jnp.arange(A_pad, dtype=i32)
    e_sorted, a_sorted, w_sorted = lax.sort((e_flat, a_ids, w_flat), num_keys=1, is_stable=True)
    tok_sorted = jnp.where(a_sorted < A, a_sorted // TOP_K, 0)
    eids = jnp.arange(E, dtype=i32)
    start = jnp.sum(e_sorted[None, :] < eids[:, None], axis=1).astype(i32)
    counts = jnp.sum(e_sorted[None, :] == eids[:, None], axis=1).astype(i32)
    end = start + counts
    f_tile = start // tm
    n_pairs_e = jnp.where(counts > 0, (end - 1) // tm - f_tile + 1, 0)
    pair_end = jnp.cumsum(n_pairs_e)
    pair_off = pair_end - n_pairs_e
    n_pairs = pair_end[-1]
    p_max = A_pad // tm + E
    p = jnp.arange(p_max, dtype=i32)
    p_eff = jnp.minimum(p, n_pairs - 1)
    e_p = jnp.sum(p_eff[:, None] >= pair_end[None, :], axis=1).astype(i32)
    tile = f_tile[e_p] + p_eff - pair_off[e_p]
    valid = p < n_pairs
    lo = jnp.where(valid, jnp.maximum(start[e_p], tile * tm) - tile * tm, 0).astype(i32)
    hi = jnp.where(valid, jnp.minimum(end[e_p], (tile + 1) * tm) - tile * tm, 0).astype(i32)
    first = (jnp.concatenate([jnp.ones((1,), bool), tile[1:] != tile[:-1]]) & valid).astype(i32)
    pairs = (tile.astype(i32), e_p, lo, hi, first, n_pairs.reshape(1).astype(i32))

    x_sorted = h2[tok_sorted]
    y_sorted = moe_grouped(x_sorted, w_sorted[:, None], pairs, prm['w_exp_gate'], prm['w_exp_up'],
                           prm['w_exp_down'], layer, tm)
    _, inv = lax.sort((a_sorted, a_ids), num_keys=1)
    routed = jnp.sum(y_sorted[inv[:A].reshape(N, TOP_K)], axis=1)
    outs, o = [], 0
    for n in sizes:
        outs.append(routed[o:o + n])
        o += n
    return outs


def _softmax_rows(s, mask):
    s = jnp.where(mask, s, NEG_INF)
    m = jnp.max(s, axis=-1, keepdims=True)
    p = jnp.where(mask, jnp.exp(s - m), 0.0)
    return p / jnp.maximum(jnp.sum(p, axis=-1, keepdims=True), 1e-30)


def _dot_nt(a, b):
    return lax.dot_general(a, b, (((1,), (1,)), ((), ())), preferred_element_type=F32)


def _nsa_kernel(q_ref, gz_ref, kc_ref, vc_ref, ks_ref, vs_ref, kw_ref, vw_ref, bc_ref, bs_ref, bw_ref, covt_ref,
                o_ref, s_scr, m_scr, l_scr, acc_scr, *, R, bq, ns, n_top):
    qb = pl.program_id(2)
    s0 = pl.multiple_of(qb * bq, bq)
    rows = R * bq
    scale = HEAD_DIM ** -0.5
    q = q_ref[...]
    q_rows = jnp.concatenate([q[:, r * HEAD_DIM:(r + 1) * HEAD_DIM] for r in range(R)], axis=0).astype(MXU_DTYPE)
    i_tok = lax.broadcasted_iota(jnp.int32, (bq, 1), 0)
    i_row = jnp.concatenate([i_tok] * R, axis=0)
    q_pos_row = s0 + i_row

    ncp = kc_ref.shape[0]
    s_c = _dot_nt(q_rows, kc_ref[...]) * scale + bc_ref[...]
    cmp_end = lax.broadcasted_iota(jnp.int32, (1, ncp), 1) * CMP_STRIDE + (CMP_LEN - 1)
    p_c = _softmax_rows(s_c, cmp_end <= q_pos_row).astype(MXU_DTYPE)
    o_c = jnp.dot(p_c, vc_ref[...], preferred_element_type=F32)

    wb = WINDOW + bq
    kw = kw_ref[pl.ds(s0, wb), :].astype(MXU_DTYPE)
    vw = vw_ref[pl.ds(s0, wb), :].astype(MXU_DTYPE)
    s_w = _dot_nt(q_rows, kw) * scale + bw_ref[...]
    j_w = lax.broadcasted_iota(jnp.int32, (1, wb), 1)
    d_w = i_row - j_w + WINDOW
    m_w = (d_w >= 0) & (d_w < WINDOW) & (s0 - WINDOW + j_w >= 0)
    p_w = _softmax_rows(s_w, m_w)
    o_w = jnp.dot(p_w.astype(MXU_DTYPE), vw, preferred_element_type=F32)

    covt = covt_ref[...]
    imp_rows = _dot_nt(covt, p_c)
    imp = imp_rows[:, 0:bq]
    for r in range(1, R):
        imp = imp + imp_rows[:, r * bq:(r + 1) * bq]
    nsr = covt.shape[0]
    blk = lax.broadcasted_iota(jnp.int32, (nsr, bq), 0)
    blk_f = blk.astype(F32)
    q_pos = s0 + lax.broadcasted_iota(jnp.int32, (1, bq), 1)
    cur = jnp.right_shift(q_pos, int(math.log2(SEL_LEN)))
    forced = (blk == 0) | ((blk <= cur) & (blk > cur - N_LOCAL_SEL))
    valid = blk * SEL_LEN <= q_pos
    score = jnp.where(forced, FORCE_SCORE, jnp.where(valid, imp, -1.0))
    score = jnp.where(blk < ns, score, -2.0)
    sel = jnp.zeros((nsr, bq), F32)
    for _ in range(n_top):
        mx = jnp.max(score, axis=0, keepdims=True)
        first = jnp.min(jnp.where(score == mx, blk_f, float(nsr)), axis=0, keepdims=True)
        hit = blk_f == first
        sel = jnp.where(hit, 1.0, sel)
        score = jnp.where(hit, -3.0e38, score)

    ch = s_scr.shape[2]
    halves = ch // Q_BLOCK
    n_chunks = (s0 + bq + ch - 1) // ch
    j_lane = lax.broadcasted_iota(jnp.int32, (1, ch), 1)
    e_row = lax.broadcasted_iota(jnp.int32, (nsr, ch), 0)
    e_col = jnp.right_shift(lax.broadcasted_iota(jnp.int32, (nsr, ch), 1), int(math.log2(SEL_LEN)))
    blocks_per_chunk = ch // SEL_LEN
    m_scr[...] = jnp.full(m_scr.shape, NEG_INF, F32)

    def scores(c, near):
        k0 = pl.multiple_of(c * ch, ch)
        k = ks_ref[pl.ds(k0, ch), :].astype(MXU_DTYPE)
        tiles = []
        for hf in range(halves):
            d = qb - (c * halves + hf)
            tiles.append(jnp.where(d == 0, bs_ref[0], jnp.where(d == 1, bs_ref[1], bs_ref[2])) if near
                         else bs_ref[2])
        s = _dot_nt(q_rows, k) * scale + jnp.concatenate(tiles, axis=1)
        expand = (e_row == c * blocks_per_chunk + e_col).astype(F32)
        picked = lax.dot_general(sel, expand, (((0,), (0,)), ((), ())), preferred_element_type=F32)
        picked = jnp.concatenate([picked] * R, axis=0)
        ok = picked > 0.5
        if near:
            ok = ok & (k0 + j_lane <= q_pos_row)
        s = jnp.where(ok, s, NEG_INF)
        s_scr[c] = s
        m_scr[...] = jnp.maximum(m_scr[...], s)

    def pass1_far(c, carry):
        scores(c, False)
        return carry

    def pass1_near(c, carry):
        scores(c, True)
        return carry

    n_far = jnp.maximum(qb - 1, 0) // halves
    lax.fori_loop(0, n_far, pass1_far, 0)
    lax.fori_loop(n_far, n_chunks, pass1_near, 0)
    m_s = jnp.max(m_scr[...], axis=-1, keepdims=True)
    l_scr[...] = jnp.zeros(l_scr.shape, F32)
    acc_scr[...] = jnp.zeros(acc_scr.shape, F32)

    def pass2(c, carry):
        p = jnp.exp(s_scr[c] - m_s)
        l_scr[...] += p
        s_scr[c] = p
        return carry

    lax.fori_loop(0, n_chunks, pass2, 0)
    inv_l = 1.0 / jnp.maximum(jnp.sum(l_scr[...], axis=-1, keepdims=True), 1e-30)

    def pass3(c, carry):
        k0 = pl.multiple_of(c * ch, ch)
        p = (s_scr[c] * inv_l).astype(MXU_DTYPE)
        acc_scr[...] += jnp.dot(p, vs_ref[pl.ds(k0, ch), :].astype(MXU_DTYPE), preferred_element_type=F32)
        return carry

    lax.fori_loop(0, n_chunks, pass3, 0)
    o_s = acc_scr[...]

    g = jax.nn.sigmoid(gz_ref[...])
    outs = []
    for r in range(R):
        sl = slice(r * bq, (r + 1) * bq)
        outs.append(g[:, 3 * r:3 * r + 1] * o_c[sl] + g[:, 3 * r + 1:3 * r + 2] * o_s[sl]
                    + g[:, 3 * r + 2:3 * r + 3] * o_w[sl])
    o_ref[...] = jnp.concatenate(outs, axis=1).astype(o_ref.dtype)


def _rel_bucket(dist):
    n = jnp.maximum(dist, 0)
    max_exact = N_BUCKETS // 2
    large = max_exact + (jnp.log(jnp.maximum(n, 1).astype(F32) / max_exact)
                         / math.log(MAX_DISTANCE / max_exact) * (N_BUCKETS - max_exact)).astype(jnp.int32)
    large = jnp.minimum(large, N_BUCKETS - 1)
    return jnp.where(n < max_exact, n, large)


def _bias_tiles(tab, dist):
    onehot = (_rel_bucket(dist)[..., None] == jnp.arange(N_BUCKETS)).astype(F32)
    t = jnp.einsum('...b,bgr->...gr', onehot, tab, precision=lax.Precision.HIGHEST)
    nd = dist.ndim
    perm = (nd,) + tuple(range(nd - 2)) + (nd + 1, nd - 2, nd - 1)
    t = jnp.transpose(t, perm)
    return t.reshape(t.shape[:-3] + (t.shape[-3] * t.shape[-2], t.shape[-1]))


def _cover(ncp, nsp):
    c0 = jnp.arange(ncp)[:, None] * CMP_STRIDE
    s0 = jnp.arange(nsp)[None, :] * SEL_LEN
    ov = jnp.minimum(c0 + CMP_LEN, s0 + SEL_LEN) - jnp.maximum(c0, s0)
    return (jnp.maximum(ov, 0).astype(F32) / CMP_LEN).astype(MXU_DTYPE)


def _compress(kv_cmp, cmp_pe, cmp_w1, cmp_w2):
    B, L = kv_cmp.shape[:2]
    r = CMP_LEN // CMP_STRIDE
    nch = L // CMP_STRIDE
    nc = nch - r + 1
    chunks = kv_cmp[:, :nch * CMP_STRIDE].reshape(B, nch, CMP_STRIDE, 2, N_KV, HEAD_DIM)
    blocks = jnp.concatenate([chunks[:, m:m + nc] for m in range(r)], axis=2)
    blocks = blocks + cmp_pe[None, None, :, :, None, :]
    flat = jnp.transpose(blocks, (0, 1, 3, 4, 2, 5)).reshape(B, nc, 2, N_KV, CMP_LEN * HEAD_DIM)
    rnd = lambda v: v.astype(MXU_DTYPE).astype(F32)
    hid = jax.nn.silu(jnp.einsum('bnsgf,sfe->bnsge', rnd(flat), rnd(cmp_w1)))
    comp = jnp.einsum('bnsge,sed->bnsgd', rnd(hid), rnd(cmp_w2))
    return comp[:, :, 0], comp[:, :, 1]


def nsa_prompt(z, B, T, lay, tabs, cmp_pe, cmp_w1, cmp_w2):
    R, G = lay['R'], N_KV
    bq = Q_BLOCK
    nqb = T // bq
    N = B * T
    rows = R * bq
    kv = z[:, lay['okv']:lay['okv'] + 6 * G * HEAD_DIM].reshape(B, T, 6, G, HEAD_DIM)
    kcb, vcb = _compress(kv[:, :, :2], cmp_pe, cmp_w1, cmp_w2)
    nc = kcb.shape[1]
    ncp = tabs['ncp']
    pad = ((0, 0), (0, 0), (0, ncp - nc), (0, 0))
    kcb = jnp.pad(jnp.transpose(kcb, (0, 2, 1, 3)), pad).astype(MXU_DTYPE)
    vcb = jnp.pad(jnp.transpose(vcb, (0, 2, 1, 3)), pad).astype(MXU_DTYPE)
    win = kv[:, :, 4:].reshape(B, T, 2 * G * HEAD_DIM)
    win_pad = jnp.concatenate([jnp.zeros((B, WINDOW, 2 * G * HEAD_DIM), F32), win], axis=1)
    gz = z[:, lay['og']:lay['og'] + 3 * G * R].reshape(N, G, 3 * R)
    gz = jnp.pad(jnp.transpose(gz, (1, 0, 2)), ((0, 0), (0, 0), (0, LANE - 3 * R)))
    ns = -(-T // SEL_LEN)
    nsr = tabs['cover_t'].shape[0]
    ch = SEL_CHUNK if T % SEL_CHUNK == 0 else Q_BLOCK
    qcol = lay['oq'] // (R * HEAD_DIM)
    kcol = lay['okv'] // HEAD_DIM
    wb = WINDOW + bq
    kernel = functools.partial(_nsa_kernel, R=R, bq=bq, ns=ns, n_top=min(SEL_TOP, ns))
    return pl.pallas_call(
        kernel,
        grid=(B, G, nqb),
        in_specs=[
            pl.BlockSpec((bq, R * HEAD_DIM), lambda b, g, i: (b * nqb + i, qcol + g)),
            pl.BlockSpec((None, bq, LANE), lambda b, g, i: (g, b * nqb + i, 0)),
            pl.BlockSpec((None, None, ncp, HEAD_DIM), lambda b, g, i: (b, g, 0, 0)),
            pl.BlockSpec((None, None, ncp, HEAD_DIM), lambda b, g, i: (b, g, 0, 0)),
            pl.BlockSpec((T, HEAD_DIM), lambda b, g, i: (b, kcol + 2 * G + g)),
            pl.BlockSpec((T, HEAD_DIM), lambda b, g, i: (b, kcol + 3 * G + g)),
            pl.BlockSpec((None, T + WINDOW, HEAD_DIM), lambda b, g, i: (b, 0, g)),
            pl.BlockSpec((None, T + WINDOW, HEAD_DIM), lambda b, g, i: (b, 0, G + g)),
            pl.BlockSpec((None, None, rows, ncp), lambda b, g, i: (g, i, 0, 0)),
            pl.BlockSpec((None, 3, rows, Q_BLOCK), lambda b, g, i: (g, 0, 0, 0)),
            pl.BlockSpec((None, rows, wb), lambda b, g, i: (g, 0, 0)),
            pl.BlockSpec((nsr, ncp), lambda b, g, i: (0, 0)),
        ],
        out_specs=pl.BlockSpec((bq, R * HEAD_DIM), lambda b, g, i: (b * nqb + i, g)),
        out_shape=jax.ShapeDtypeStruct((N, G * R * HEAD_DIM), MXU_DTYPE),
        scratch_shapes=[pltpu.VMEM((T // ch, rows, ch), F32), pltpu.VMEM((rows, ch), F32),
                        pltpu.VMEM((rows, ch), F32), pltpu.VMEM((rows, HEAD_DIM), F32)],
        compiler_params=_cparams(("parallel", "parallel", "arbitrary")),
    )(z, gz, kcb, vcb, z, z, win_pad, win_pad, tabs['bias_c'], tabs['bias_s'], tabs['bias_w'], tabs['cover_t'])


def nsa_prompt_tables(rel_bias, T, R):
    G = N_KV
    bq = Q_BLOCK
    nqb = T // bq
    lut = rel_bias.astype(F32).reshape(N_BUCKETS, G, R)
    nc = T // CMP_STRIDE - CMP_LEN // CMP_STRIDE + 1
    ncp = -(-nc // LANE) * LANE
    ns = -(-T // SEL_LEN)
    nsr = -(-ns // SUBLANE) * SUBLANE
    i = jnp.arange(bq)[:, None]
    t = jnp.arange(T).reshape(nqb, bq, 1)
    cmp_end = (jnp.arange(ncp) * CMP_STRIDE + CMP_LEN - 1)[None, None, :]
    bias_c = _bias_tiles(lut, t - cmp_end)
    j = jnp.arange(Q_BLOCK)[None, :]
    d_s = jnp.stack([i - j, Q_BLOCK + i - j, jnp.full((bq, Q_BLOCK), MAX_DISTANCE)])
    bias_s = _bias_tiles(lut, d_s)
    jw = jnp.arange(WINDOW + bq)[None, :]
    bias_w = _bias_tiles(lut, i - jw + WINDOW)
    return dict(bias_c=bias_c, bias_s=bias_s, bias_w=bias_w, cover_t=_cover(ncp, nsr).T, ncp=ncp)


ROWS_PER_TOKEN = 4 * N_KV
CMP_PAGES_PER_STEP = 8
ATT_PAGES_PER_STEP = 8


def _cmp_kernel(pt_ref, *refs, n_pg, chunks_per_page):
    page_refs = refs[:n_pg]
    pe_ref, w1_ref, o_ref, lhs_scr = refs[n_pg], refs[n_pg + 1], refs[n_pg + 2], refs[n_pg + 3]
    cpp = chunks_per_page
    stride = CMP_STRIDE * ROWS_PER_TOKEN
    for k in range(n_pg):
        for s in range(2):
            for g in range(N_KV):
                r0 = g * n_pg * cpp + k * cpp
                for j in range(CMP_STRIDE):
                    piece = page_refs[k][pl.ds(j * ROWS_PER_TOKEN + s * N_KV + g, cpp, stride=stride), :]
                    for h in range(2):
                        lhs_scr[h, s, r0:r0 + cpp, j * HEAD_DIM:(j + 1) * HEAD_DIM] = (
                            piece + pe_ref[h * CMP_STRIDE + j, s:s + 1, :])
    half = CMP_STRIDE * HEAD_DIM
    nrow = n_pg * cpp
    for s in range(2):
        top = jnp.dot(lhs_scr[0, s].astype(MXU_DTYPE), w1_ref[s, :half, :].astype(MXU_DTYPE),
                      preferred_element_type=F32)
        bot = jnp.dot(lhs_scr[1, s].astype(MXU_DTYPE), w1_ref[s, half:, :].astype(MXU_DTYPE),
                      preferred_element_type=F32)
        for g in range(N_KV):
            o_ref[s, 0, g] = top[g * nrow:(g + 1) * nrow]
            o_ref[s, 1, g] = bot[g * nrow:(g + 1) * nrow]


def decode_compress(cache2, page_table, cmp_pe, cmp_w1, layer):
    bs, n_pages = page_table.shape
    prow = cache2.shape[2]
    cpp = prow // ROWS_PER_TOKEN // CMP_STRIDE
    n_pg = math.gcd(CMP_PAGES_PER_STEP, n_pages)
    nch = n_pages * cpp
    page_specs = [pl.BlockSpec((None, None, prow, HEAD_DIM),
                               functools.partial(lambda b, i, pt, k: (layer, pt[b, i * n_pg + k], 0, 0), k=k))
                  for k in range(n_pg)]
    grid_spec = pltpu.PrefetchScalarGridSpec(
        num_scalar_prefetch=1,
        grid=(bs, n_pages // n_pg),
        in_specs=page_specs + [pl.BlockSpec((None, CMP_LEN, 2, HEAD_DIM), lambda b, i, pt: (layer, 0, 0, 0)),
                               pl.BlockSpec((None, 2, CMP_LEN * HEAD_DIM, HEAD_DIM), lambda b, i, pt: (layer, 0, 0, 0))],
        out_specs=pl.BlockSpec((None, 2, 2, N_KV, n_pg * cpp, HEAD_DIM), lambda b, i, pt: (b, 0, 0, 0, i, 0)),
        scratch_shapes=[pltpu.VMEM((2, 2, N_KV * n_pg * cpp, CMP_STRIDE * HEAD_DIM), F32)],
    )
    return pl.pallas_call(
        functools.partial(_cmp_kernel, n_pg=n_pg, chunks_per_page=cpp),
        grid_spec=grid_spec,
        out_shape=jax.ShapeDtypeStruct((bs, 2, 2, N_KV, nch, HEAD_DIM), F32),
        compiler_params=_cparams(("parallel", "arbitrary")),
    )(page_table, *([cache2] * n_pg), cmp_pe, cmp_w1)


def _decode_kernel(pt_ref, z_ref, rt_ref, parts_ref, *refs, lay, T, R, past_len, n_pg, ns, n_top, wb):
    page_refs = refs[:n_pg]
    (win_ref, w2_ref, bc_ref, bs_ref, bw_ref, cov_ref, o_ref, q_scr, new_scr, sel_scr, m_scr, l_scr, acc_scr,
     oc_scr, st_scr) = refs[n_pg:]
    p = pl.program_id(1)
    G = N_KV
    rows = R * T
    scale = HEAD_DIM ** -0.5
    page = page_refs[0].shape[0] // ROWS_PER_TOKEN
    span = n_pg * page
    t_row = rt_ref[...]
    q_pos_row = past_len + t_row
    nsp = cov_ref.shape[1]
    kvw = N_KV * HEAD_DIM

    @pl.when(p == 0)
    def _init():
        z = z_ref[...]
        new_scr[...] = jnp.zeros(new_scr.shape, F32)
        for g in range(G):
            for r in range(R):
                c0 = lay['oq'] + (g * R + r) * HEAD_DIM
                q_scr[g, r * T:(r + 1) * T, :] = z[:, c0:c0 + HEAD_DIM]
            for si, s in enumerate((2, 3, 4, 5)):
                c0 = lay['okv'] + s * kvw + g * HEAD_DIM
                new_scr[si, g, 0:T, :] = z[:, c0:c0 + HEAD_DIM]
        nch = parts_ref.shape[3]
        cmp_end = lax.broadcasted_iota(jnp.int32, (1, nch), 1) * CMP_STRIDE + (CMP_LEN - 1)
        blk = lax.broadcasted_iota(jnp.int32, (rows, nsp), 1)
        blk_f = blk.astype(F32)
        cur = jnp.right_shift(q_pos_row, int(math.log2(SEL_LEN)))
        forced = (blk == 0) | ((blk <= cur) & (blk > cur - N_LOCAL_SEL))
        valid = blk * SEL_LEN <= q_pos_row
        cov = cov_ref[...]
        for g in range(G):
            comp = []
            for s in range(2):
                pre = parts_ref[s, 0, g] + pltpu.roll(parts_ref[s, 1, g], nch - 1, 0)
                hid = pre * jax.nn.sigmoid(pre)
                comp.append(jnp.dot(hid.astype(MXU_DTYPE), w2_ref[s].astype(MXU_DTYPE),
                                    preferred_element_type=F32).astype(MXU_DTYPE))
            qr = q_scr[g].astype(MXU_DTYPE)
            s_c = _dot_nt(qr, comp[0]) * scale + bc_ref[g]
            p_c = _softmax_rows(s_c, cmp_end <= q_pos_row).astype(MXU_DTYPE)
            oc_scr[g] = jnp.dot(p_c, comp[1], preferred_element_type=F32)
            imp = jnp.dot(p_c, cov, preferred_element_type=F32)
            imp_all = imp
            for r in range(1, R):
                imp_all = imp_all + pltpu.roll(imp, r * T, 0)
            score = jnp.where(forced, FORCE_SCORE, jnp.where(valid, imp_all, -1.0))
            score = jnp.where(blk < ns, score, -2.0)
            sel = jnp.zeros((rows, nsp), F32)
            for _ in range(n_top):
                mx = jnp.max(score, axis=-1, keepdims=True)
                first = jnp.min(jnp.where(score == mx, blk_f, float(nsp)), axis=-1, keepdims=True)
                hit = blk_f == first
                sel = jnp.where(hit, 1.0, sel)
                score = jnp.where(hit, -3.0e38, score)
            sel_scr[g] = sel
        m_scr[...] = jnp.full(m_scr.shape, NEG_INF, F32)
        l_scr[...] = jnp.zeros(l_scr.shape, F32)
        acc_scr[...] = jnp.zeros(acc_scr.shape, F32)

    def flash_update(g, s, ok, v):
        s = jnp.where(ok, s, NEG_INF)
        m_old = m_scr[g]
        m_new = jnp.maximum(m_old, jnp.max(s, axis=-1, keepdims=True))
        alpha = jnp.exp(m_old - m_new)
        pe = jnp.where(ok, jnp.exp(s - m_new), 0.0)
        l_scr[g] = alpha * l_scr[g] + jnp.sum(pe, axis=-1, keepdims=True)
        acc_scr[g] = alpha * acc_scr[g] + jnp.dot(pe.astype(MXU_DTYPE), v, preferred_element_type=F32)
        m_scr[g] = m_new

    e_row = lax.broadcasted_iota(jnp.int32, (nsp, span), 0)
    e_col = jnp.right_shift(lax.broadcasted_iota(jnp.int32, (nsp, span), 1), int(math.log2(SEL_LEN)))
    expand = (e_row == p * (span // SEL_LEN) + e_col).astype(MXU_DTYPE)
    last = p == pl.num_programs(1) - 1
    for g in range(G):
        k = jnp.concatenate([pr[pl.ds(2 * N_KV + g, page, stride=ROWS_PER_TOKEN), :] for pr in page_refs],
                            axis=0).astype(MXU_DTYPE)
        v = jnp.concatenate([pr[pl.ds(3 * N_KV + g, page, stride=ROWS_PER_TOKEN), :] for pr in page_refs],
                            axis=0).astype(MXU_DTYPE)
        qr = q_scr[g].astype(MXU_DTYPE)
        far = [bs_ref[g, 2]] * (n_pg - 1)
        bias = jnp.where(last, jnp.concatenate(far + [bs_ref[g, 1]], axis=1),
                         jnp.concatenate(far + [bs_ref[g, 2]], axis=1))
        s = _dot_nt(qr, k) * scale + bias
        picked = jnp.dot(sel_scr[g].astype(MXU_DTYPE), expand, preferred_element_type=F32)
        flash_update(g, s, picked > 0.5, v)

    @pl.when(last)
    def _finish():
        z = z_ref[...]
        j_new = lax.broadcasted_iota(jnp.int32, (1, new_scr.shape[2]), 1)
        ok_new = (j_new <= t_row) & (j_new < T)
        j_w = lax.broadcasted_iota(jnp.int32, (1, wb), 1)
        d_w = wb + t_row - j_w
        ok_w = (d_w >= 0) & (d_w < WINDOW) & (past_len - wb + j_w >= 0)
        gates = jax.nn.sigmoid(z[:, lay['og']:lay['og'] + LANE])
        for g in range(G):
            qr = q_scr[g].astype(MXU_DTYPE)
            s_n = _dot_nt(qr, new_scr[0, g].astype(MXU_DTYPE)) * scale + bs_ref[g, 0]
            flash_update(g, s_n, ok_new, new_scr[1, g].astype(MXU_DTYPE))
            o_s = acc_scr[g] / jnp.maximum(l_scr[g], 1e-30)
            kw = win_ref[pl.ds(g, wb, stride=2 * N_KV), :].astype(MXU_DTYPE)
            vw = win_ref[pl.ds(N_KV + g, wb, stride=2 * N_KV), :].astype(MXU_DTYPE)
            s1 = jnp.where(ok_w, _dot_nt(qr, kw) * scale + bw_ref[g], NEG_INF)
            s2 = jnp.where(ok_new, _dot_nt(qr, new_scr[2, g].astype(MXU_DTYPE)) * scale + bs_ref[g, 0], NEG_INF)
            mw = jnp.maximum(jnp.max(s1, axis=-1, keepdims=True), jnp.max(s2, axis=-1, keepdims=True))
            e1 = jnp.where(ok_w, jnp.exp(s1 - mw), 0.0)
            e2 = jnp.where(ok_new, jnp.exp(s2 - mw), 0.0)
            den = jnp.sum(e1, axis=-1, keepdims=True) + jnp.sum(e2, axis=-1, keepdims=True)
            o_w = (jnp.dot(e1.astype(MXU_DTYPE), vw, preferred_element_type=F32)
                   + jnp.dot(e2.astype(MXU_DTYPE), new_scr[3, g].astype(MXU_DTYPE), preferred_element_type=F32))
            o_w = o_w / jnp.maximum(den, 1e-30)
            st_scr[0] = oc_scr[g]
            st_scr[1] = o_s
            st_scr[2] = o_w
            for r in range(R):
                gc = (g * R + r) * 3
                rs = pl.ds(r * T, T)
                out = (gates[:, gc:gc + 1] * st_scr[0, rs, :] + gates[:, gc + 1:gc + 2] * st_scr[1, rs, :]
                       + gates[:, gc + 2:gc + 3] * st_scr[2, rs, :])
                c0 = (g * R + r) * HEAD_DIM
                o_ref[:, c0:c0 + HEAD_DIM] = out


def nsa_decode(zs3, cache2, page_table, win2, lay, tabs, cmp_pe, cmp_w1, cmp_w2, layer, past_len):
    bs, T, NZ = zs3.shape
    R, G = lay['R'], N_KV
    rows = R * T
    n_pages = page_table.shape[1]
    prow = cache2.shape[2]
    page = prow // ROWS_PER_TOKEN
    assert past_len % page == 0 and page % SEL_LEN == 0 and T < CMP_STRIDE and T <= SEL_LEN
    parts = decode_compress(cache2, page_table, cmp_pe, cmp_w1, layer)
    nch = parts.shape[4]
    wb = win2.shape[2] // (2 * N_KV)
    ns = -(-(past_len + T) // SEL_LEN)
    nsp = tabs['cover'].shape[1]
    npad = tabs['bias_s'].shape[-1]
    n_pg = math.gcd(ATT_PAGES_PER_STEP, n_pages)
    kernel = functools.partial(_decode_kernel, lay=lay, T=T, R=R, past_len=past_len, n_pg=n_pg, ns=ns,
                               n_top=min(SEL_TOP, ns), wb=wb)
    page_specs = [pl.BlockSpec((None, None, prow, HEAD_DIM),
                               functools.partial(lambda b, p, pt, k: (layer, pt[b, p * n_pg + k], 0, 0), k=k))
                  for k in range(n_pg)]
    grid_spec = pltpu.PrefetchScalarGridSpec(
        num_scalar_prefetch=1,
        grid=(bs, n_pages // n_pg),
        in_specs=[
            pl.BlockSpec((None, T, NZ), lambda b, p, pt: (b, 0, 0)),
            pl.BlockSpec((rows, 1), lambda b, p, pt: (0, 0)),
            pl.BlockSpec((None, 2, 2, G, nch, HEAD_DIM), lambda b, p, pt: (b, 0, 0, 0, 0, 0))]
        + page_specs + [
            pl.BlockSpec((None, None, wb * 2 * N_KV, HEAD_DIM), lambda b, p, pt: (layer, b, 0, 0)),
            pl.BlockSpec((None, 2, HEAD_DIM, HEAD_DIM), lambda b, p, pt: (layer, 0, 0, 0)),
            pl.BlockSpec((G, rows, nch), lambda b, p, pt: (0, 0, 0)),
            pl.BlockSpec((G, 3, rows, npad), lambda b, p, pt: (0, 0, 0, 0)),
            pl.BlockSpec((G, rows, wb), lambda b, p, pt: (0, 0, 0)),
            pl.BlockSpec((nch, nsp), lambda b, p, pt: (0, 0)),
        ],
        out_specs=pl.BlockSpec((None, T, G * R * HEAD_DIM), lambda b, p, pt: (b, 0, 0)),
        scratch_shapes=[pltpu.VMEM((G, rows, HEAD_DIM), F32), pltpu.VMEM((4, G, npad, HEAD_DIM), F32),
                        pltpu.VMEM((G, rows, nsp), F32), pltpu.VMEM((G, rows, 1), F32),
                        pltpu.VMEM((G, rows, 1), F32), pltpu.VMEM((G, rows, HEAD_DIM), F32),
                        pltpu.VMEM((G, rows, HEAD_DIM), F32), pltpu.VMEM((3, rows, HEAD_DIM), F32)],
    )
    return pl.pallas_call(
        kernel,
        grid_spec=grid_spec,
        out_shape=jax.ShapeDtypeStruct((bs, T, G * R * HEAD_DIM), F32),
        compiler_params=_cparams(("parallel", "arbitrary")),
    )(page_table, zs3, tabs['row_t'], parts, *([cache2] * n_pg), win2, cmp_w2, tabs['bias_c'], tabs['bias_s'],
      tabs['bias_w'], tabs['cover'])


def nsa_decode_tables(rel_bias, T, R, past_len, wb, page):
    G = N_KV
    tab = rel_bias.astype(F32).reshape(N_BUCKETS, G, R)
    nch = past_len // CMP_STRIDE
    ns = -(-(past_len + T) // SEL_LEN)
    nsp = -(-ns // LANE) * LANE
    t = jnp.arange(T)[:, None]
    cmp_end = (jnp.arange(nch) * CMP_STRIDE + CMP_LEN - 1)[None, :]
    bias_c = _bias_tiles(tab, past_len + t - cmp_end)
    j = jnp.arange(page)[None, :]
    d_s = jnp.stack([t - j, page + t - j, jnp.full((T, page), MAX_DISTANCE)])
    bias_s = _bias_tiles(tab, d_s)
    jw = jnp.arange(wb)[None, :]
    bias_w = _bias_tiles(tab, wb + t - jw)
    row_t = jnp.tile(jnp.arange(T, dtype=jnp.int32), R).reshape(R * T, 1)
    return dict(bias_c=bias_c, bias_s=bias_s, bias_w=bias_w, cover=_cover(nch, nsp), row_t=row_t)


CONV_HIST = 32


def _conv_mix_kernel(ab_ref, ac_ref, ah_ref, bl_ref, bg_ref, zc_ref, cap_ref, cbp_ref, pp_ref, caw_ref, cbw_ref,
                     cbb_ref, lng_ref, lnb_ref, pw_ref, ps_ref, y_ref, cao_ref, cbo_ref, po_ref,
                     ea_scr, eb_scr, ec_scr, ear_scr, ebr_scr, *, tt, pos0):
    i = pl.program_id(1)
    H = CONV_HIST
    ka, kb, kp = caw_ref.shape[0] - 1, cbw_ref.shape[0] - 1, pp_ref.shape[0]
    da, db = ab_ref.shape[1], bl_ref.shape[1]
    rnd = lambda v: v.astype(MXU_DTYPE).astype(F32)

    @pl.when(i == 0)
    def _():
        for scr, prev, k in ((ea_scr, cap_ref, ka), (eb_scr, cbp_ref, kb), (ec_scr, pp_ref, kp)):
            scr[0:H, :] = jnp.zeros((H, scr.shape[1]), F32)
            scr[H - k:H, :] = prev[...]
        for scr, prev, k in ((ear_scr, cap_ref, ka), (ebr_scr, cbp_ref, kb)):
            scr[0:H, :] = jnp.zeros((H, scr.shape[1]), F32)
            scr[H - k:H, :] = rnd(prev[...])

    ua = ac_ref[...] * ah_ref[...]
    ea_scr[H:H + tt, :] = ua
    ear_scr[H:H + tt, :] = rnd(ua)
    conv = None
    for k in range(ka + 1):
        term = rnd(caw_ref[k:k + 1, :]) * ear_scr[pl.ds(H - ka + k, tt), :]
        conv = term if conv is None else conv + term
    y_ref[:, 0:da] = (ab_ref[...] * conv).astype(y_ref.dtype)

    ub = bl_ref[...] * jax.nn.sigmoid(bg_ref[...])
    eb_scr[H:H + tt, :] = ub
    ebr_scr[H:H + tt, :] = rnd(ub)
    cb = None
    for k in range(kb + 1):
        term = rnd(cbw_ref[k:k + 1, :]) * ebr_scr[pl.ds(H - kb + k, tt), :]
        cb = term if cb is None else cb + term
    cb = cb + cbb_ref[...]
    mu = jnp.mean(cb, axis=-1, keepdims=True)
    var = jnp.mean(jnp.square(cb - mu), axis=-1, keepdims=True)
    ln = (cb - mu) * lax.rsqrt(var + LN_EPS) * lng_ref[...] + lnb_ref[...]
    y_ref[:, da:da + db] = (ln * jax.nn.sigmoid(ln)).astype(y_ref.dtype)

    zc = zc_ref[...]
    ec_scr[H:H + tt, :] = zc
    pos = pos0 + i * tt + lax.broadcasted_iota(jnp.int32, (tt, 1), 0)
    cg = zc.shape[1] // len(POOL_WINDOWS)
    for gi, w in enumerate(POOL_WINDOWS):
        c0 = gi * cg
        ssum = None
        for k in range(w):
            term = ec_scr[pl.ds(H - k, tt), c0:c0 + cg]
            ssum = term if ssum is None else ssum + term
        cnt = jnp.minimum(w, pos + 1).astype(F32)
        pooled = ssum / cnt - zc[:, c0:c0 + cg]
        mixed = jnp.dot(pooled.astype(MXU_DTYPE), pw_ref[gi].astype(MXU_DTYPE), preferred_element_type=F32)
        y_ref[:, da + db + c0:da + db + c0 + cg] = (mixed * ps_ref[:, c0:c0 + cg]).astype(y_ref.dtype)

    for scr in (ea_scr, eb_scr, ec_scr, ear_scr, ebr_scr):
        tail = scr[pl.ds(tt, H), :]
        scr[0:H, :] = tail

    @pl.when(i == pl.num_programs(1) - 1)
    def _():
        cao_ref[...] = ea_scr[H - ka:H, :]
        cbo_ref[...] = eb_scr[H - kb:H, :]
        po_ref[...] = ec_scr[H - kp:H, :]


def conv_mix(z3, ca_prev, cb_prev, pool_prev, prm, lay, layer, pos0, out_dtype):
    B, T, _ = z3.shape
    DA, DB, DC = lay['DA'], lay['DB'], lay['DC']
    assert DA == DB == DC and lay['ob'] == 3 * DA and lay['oc'] == 5 * DA
    tt = _pick(T, (256, 128, 64, 32)) if T >= CONV_HIST else T
    nt = T // tt
    ka, kb, kp = ca_prev.shape[1], cb_prev.shape[1], pool_prev.shape[1]
    ngrp = len(POOL_WINDOWS)
    zcol = lambda c: pl.BlockSpec((None, tt, DA), lambda b, i: (b, i, c))
    seq = lambda k, w: pl.BlockSpec((None, k, w), lambda b, i: (b, 0, 0))
    lyr = lambda k, w: pl.BlockSpec((None, k, w), lambda b, i: (layer, 0, 0))
    vec = lambda a: a.reshape(a.shape[0], 1, a.shape[1])
    return pl.pallas_call(
        functools.partial(_conv_mix_kernel, tt=tt, pos0=pos0),
        grid=(B, nt),
        in_specs=[zcol(0), zcol(1), zcol(2), zcol(3), zcol(4), zcol(5),
                  seq(ka, DA), seq(kb, DB), seq(kp, DC),
                  lyr(ka + 1, DA), lyr(kb + 1, DB), lyr(1, DB), lyr(1, DB), lyr(1, DB),
                  pl.BlockSpec((None, ngrp, DC // ngrp, DC // ngrp), lambda b, i: (layer, 0, 0, 0)), lyr(1, DC)],
        out_specs=[pl.BlockSpec((None, tt, DA + DB + DC), lambda b, i: (b, i, 0)),
                   seq(ka, DA), seq(kb, DB), seq(kp, DC)],
        out_shape=[jax.ShapeDtypeStruct((B, T, DA + DB + DC), out_dtype),
                   jax.ShapeDtypeStruct((B, ka, DA), F32), jax.ShapeDtypeStruct((B, kb, DB), F32),
                   jax.ShapeDtypeStruct((B, kp, DC), F32)],
        scratch_shapes=[pltpu.VMEM((CONV_HIST + tt, DA), F32), pltpu.VMEM((CONV_HIST + tt, DB), F32),
                        pltpu.VMEM((CONV_HIST + tt, DC), F32), pltpu.VMEM((CONV_HIST + tt, DA), F32),
                        pltpu.VMEM((CONV_HIST + tt, DB), F32)],
        compiler_params=_cparams(("parallel", "arbitrary")),
    )(z3, z3, z3, z3, z3, z3, ca_prev, cb_prev, pool_prev, prm['conv_a_w'], prm['conv_b_w'],
      vec(prm['conv_b_b']), vec(prm['ln_b_g']), vec(prm['ln_b_b']), prm['pool_w'], vec(prm['pool_scale']))


def _layout(prm, D):
    DA = prm['conv_a_w'].shape[-1]
    DB = prm['conv_b_w'].shape[-1]
    DC = prm['pool_scale'].shape[-1]
    NH = prm['rel_bias'].shape[1]
    R = NH // N_KV
    DD = NH * HEAD_DIM
    ob = 3 * DA
    oc = ob + 2 * DB
    oq = oc + DC
    okv = oq + DD
    og = okv + 6 * N_KV * HEAD_DIM
    assert oq % (R * HEAD_DIM) == 0 and okv % HEAD_DIM == 0
    return dict(DA=DA, DB=DB, DC=DC, R=R, DD=DD, ob=ob, oc=oc, oq=oq, okv=okv, og=og, n_in=og + 3 * NH)


def _mixer(z, B, T, l, pos0, prev, win_keep, prm, lay, tabs):
    kv_past, win_prev, ca_prev, cb_prev, pool_prev = prev
    z3 = z.reshape(B, T, z.shape[-1])
    small = T < 2 * SUBLANE
    y_abc, ca, cb, pool = conv_mix(z3, ca_prev, cb_prev, pool_prev, prm, lay, l, pos0, F32 if small else MXU_DTYPE)
    kv = z3[..., lay['okv']:lay['okv'] + 6 * N_KV * HEAD_DIM].reshape(B, T, 6, N_KV, HEAD_DIM)
    win_rows = jnp.concatenate([win_prev, kv[:, :, 4:]], axis=1)
    if kv_past is None:
        y_d = nsa_prompt(z, B, T, lay, tabs, prm['cmp_pe'][l], prm['cmp_w1'][l], prm['cmp_w2'][l])
    else:
        y_d = nsa_decode(z3, kv_past['cache2'], kv_past['page_table'], kv_past['win2'], lay, tabs,
                         prm['cmp_pe'], prm['cmp_w1'], prm['cmp_w2'], l, pos0)
    y_abc = y_abc.reshape(B * T, -1).astype(MXU_DTYPE)
    y_d = y_d.reshape(B * T, -1).astype(MXU_DTYPE)
    state = (kv[:, :, :4], win_rows[:, -win_keep:], ca, cb, pool)
    return (y_abc, y_d), state


class _Group:
    def __init__(self, x3, mods, pos0, prev_fn, win_keep, tabs, prm, lay):
        self.B, self.T, self.D = x3.shape
        self.mods, self.pos0, self.prev_fn, self.win_keep, self.tabs = mods, pos0, prev_fn, win_keep, tabs
        self.prm, self.lay = prm, lay
        N = self.B * self.T
        self.tr = _pick(N, (256, 128, 64, 32, 16, 8))
        if self.T % self.tr == 0:
            self.per_seq = lambda m: m[:, None, :]
        else:
            assert self.tr == N
            self.per_seq = lambda m: jnp.repeat(m, self.T, axis=0)[None]
        splits, acc = [], 0
        for wdt in (lay['DA'], lay['DB'], lay['DC'], lay['DD']):
            splits.append((acc, acc + wdt))
            acc += wdt
        self.splits = tuple(splits)
        self.outs = ([], [], [], [], [])
        self.x = x3.reshape(N, self.D)
        mod = mods[0]
        self.h = rms_mod(self.x, prm['norm_w'][0, 0][None], self.per_seq(1.0 + mod[:, 1]), self.per_seq(mod[:, 0]),
                         self.T, self.tr)

    def mixer_sublayer(self, l):
        prm, nw, mod, ps = self.prm, self.prm['norm_w'], self.mods[l], self.per_seq
        z = matmul(self.h, prm['w_in'], l)
        (y_abc, y_d), st = _mixer(z, self.B, self.T, l, self.pos0, self.prev_fn(l), self.win_keep, prm, self.lay,
                                  self.tabs)
        merged = merge(self.h, y_abc, y_d, prm['w_mgate'], prm['w_branch'], l, self.splits)
        y = matmul(merged, prm['w_out'], l)
        self.x, h2 = resid_norm(self.x, y, nw[l, 1][None], ps(mod[:, 2]), nw[l, 2][None],
                                ps(1.0 + mod[:, 4]), ps(mod[:, 3]), self.T, self.tr)
        for a, s in zip(self.outs, st):
            a.append(s)
        return h2

    def ffn_residual(self, l, f):
        nw, mod, ps = self.prm['norm_w'], self.mods[l], self.per_seq
        ln = min(l + 1, nw.shape[0] - 1)
        modn = self.mods[ln]
        self.x, self.h = resid_norm(self.x, f, nw[l, 3][None], ps(mod[:, 5]), nw[ln, 0][None],
                                    ps(1.0 + modn[:, 1]), ps(modn[:, 0]), self.T, self.tr)

    def result(self):
        return self.x.reshape(self.B, self.T, self.D), [jnp.stack(a) for a in self.outs]


def _run_trunks(groups, prm):
    for l in range(prm['norm_w'].shape[0]):
        h2s = [g.mixer_sublayer(l) for g in groups]
        routes = [moe_route_shared(h2, prm, l) for h2 in h2s]
        routed = moe_routed(h2s, [r[0] for r in routes], [r[1] for r in routes], prm, l)
        for g, r, rt in zip(groups, routes, routed):
            g.ffn_residual(l, rt + r[2])
    return [g.result() for g in groups]


def kernel(x_prompt, x_sample, cache_kv, state_kv_win, state_conv_a, state_conv_b, state_pool, page_table,
           c_prompt, c_sample, rel_bias, ada_w, ada_b, norm_w, w_in, conv_a_w, conv_b_w, conv_b_b, ln_b_g, ln_b_b,
           pool_w, pool_scale, cmp_pe, cmp_w1, cmp_w2, w_branch, w_mgate, w_out, w_router, b_router,
           w_exp_gate, w_exp_up, w_exp_down, w_sh_gate, w_sh_up, w_sh_down):
    bp, tp, D = x_prompt.shape
    bs, ts, _ = x_sample.shape
    depth = norm_w.shape[0]
    dt = x_prompt.dtype
    past_len = page_table.shape[1] * cache_kv.shape[2]
    E = b_router.shape[-1]

    prm = dict(rel_bias=rel_bias, norm_w=norm_w, conv_a_w=conv_a_w, conv_b_w=conv_b_w, conv_b_b=conv_b_b,
               ln_b_g=ln_b_g, ln_b_b=ln_b_b, pool_w=pool_w, pool_scale=pool_scale, cmp_pe=cmp_pe, cmp_w1=cmp_w1,
               cmp_w2=cmp_w2, b_router=b_router)
    lay = _layout(prm, D)
    n_in = lay['n_in']
    nz = lay['og'] + 2 * LANE
    prm['w_in'] = jnp.pad(w_in, ((0, 0), (0, 0), (0, nz - n_in))).astype(MXU_DTYPE)
    prm['w_mgate'] = w_mgate.astype(MXU_DTYPE)
    prm['w_branch'] = w_branch.astype(MXU_DTYPE)
    prm['w_out'] = w_out.astype(MXU_DTYPE)
    prm['w_router'] = jnp.pad(w_router, ((0, 0), (0, 0), (0, LANE - E))).astype(MXU_DTYPE)
    prm['w_exp_gate'] = w_exp_gate.astype(MXU_DTYPE)
    prm['w_exp_up'] = w_exp_up.astype(MXU_DTYPE)
    prm['w_exp_down'] = w_exp_down.astype(MXU_DTYPE)
    prm['w_sh_gu'] = jnp.concatenate([w_sh_gate, w_sh_up], axis=-1).astype(MXU_DTYPE)
    prm['w_sh_down'] = w_sh_down.astype(MXU_DTYPE)

    nb = bp + bs
    nbp = -(-nb // 16) * 16
    cs = jnp.pad(jax.nn.silu(jnp.concatenate([c_prompt, c_sample], axis=0)), ((0, nbp - nb), (0, 0)))
    cs = cs.astype(MXU_DTYPE)
    mods_p, mods_s = [], []
    for l in range(depth):
        m = (matmul(cs, ada_w, l, tn=_pick(6 * D, (512, 256, 128)))[:nb] + ada_b[l]).reshape(nb, 6, D)
        mods_p.append(m[:bp])
        mods_s.append(m[bp:])

    tabs = nsa_prompt_tables(rel_bias, tp, lay['R'])

    def prompt_prev(l):
        return (None,
                jnp.zeros((bp, WINDOW, 2, N_KV, HEAD_DIM), dt),
                jnp.zeros((bp, conv_a_w.shape[1] - 1, lay['DA']), dt),
                jnp.zeros((bp, conv_b_w.shape[1] - 1, lay['DB']), dt),
                jnp.zeros((bp, state_pool.shape[2], lay['DC']), dt))

    page = cache_kv.shape[2]
    wbuf = state_kv_win.shape[2]
    paged = dict(cache2=cache_kv.reshape(depth, cache_kv.shape[1], page * ROWS_PER_TOKEN, HEAD_DIM),
                 win2=state_kv_win.reshape(depth, bs, wbuf * 2 * N_KV, HEAD_DIM), page_table=page_table)
    tabs_s = nsa_decode_tables(rel_bias, ts, lay['R'], past_len, wbuf, page)

    def sample_prev(l):
        return (paged, state_kv_win[l], state_conv_a[l], state_conv_b[l], state_pool[l])

    groups = [_Group(x_prompt, mods_p, 0, prompt_prev, min(WINDOW, tp), tabs, prm, lay),
              _Group(x_sample, mods_s, past_len, sample_prev, wbuf, tabs_s, prm, lay)]
    (y_prompt, st_p), (y_sample, st_s) = _run_trunks(groups, prm)
    kv_p, win_p, ca_p, cb_p, pl_p = st_p
    kv_s, win_s, ca_s, cb_s, pl_s = st_s
    return (y_prompt, y_sample, kv_p, kv_s, win_p, win_s, ca_p, ca_s, cb_p, cb_s, pl_p, pl_s)
```
